```python
import math
import jax, jax.numpy as jnp
from jax import lax
import numpy as np

D_MODEL = 2048
BATCH = 2
SEQ = 4096
DEPTH = 1
DEC_BATCH = 128
DEC_SEQ = 1
PAST_LEN = 2048
PAGE_SIZE = 128

H_RET = 4
DK_RET = 256
DV_RET = 256
W_RET = H_RET * DV_RET
H_DIFF = 4
DK_DIFF = 128
DV_DIFF = 256
W_DIFF = H_DIFF * DV_DIFF
RET_CHUNK = 128
Q_BLOCK = 128
ROPE_BASE = 10000.0
DEEPNORM_ALPHA = (2.0 * DEPTH) ** 0.25
DEEPNORM_BETA = (8.0 * DEPTH) ** -0.25
LN_EPS = 1e-5
GN_EPS = 1e-6
NEG_INF = -1e30
IN_SIZES = (H_RET * DK_RET, H_RET * DK_RET, W_RET, W_RET,
            H_DIFF * 2 * DK_DIFF, H_DIFF * 2 * DK_DIFF, W_DIFF, W_DIFF,
            D_MODEL, D_MODEL)
D_IN = sum(IN_SIZES)

kernel_name = "retnet_diffattn_gated_hybrid_step"


def _rope(x, pos):
    half = x.shape[-1] // 2
    inv_freq = 1.0 / (ROPE_BASE ** (jnp.arange(half, dtype=jnp.float32) / half))
    ang = pos[:, None] * inv_freq[None, :]
    cos = jnp.cos(ang)[:, None, :].astype(x.dtype)
    sin = jnp.sin(ang)[:, None, :].astype(x.dtype)
    x1, x2 = x[..., :half], x[..., half:]
    return jnp.concatenate([x1 * cos - x2 * sin, x2 * cos + x1 * sin], axis=-1)


def _in_proj(x, w_in, pos):
    B, T = x.shape[0], x.shape[1]
    h = x @ w_in
    q_r, k_r, v_r, z_r, q_a, k_a, v_a, z_a, g_r, g_a = jnp.split(
        h, np.cumsum(IN_SIZES)[:-1].tolist(), axis=-1)
    q_r = _rope(q_r.reshape(B, T, H_RET, DK_RET), pos)
    k_r = _rope(k_r.reshape(B, T, H_RET, DK_RET), pos) * (DK_RET ** -0.5)
    v_r = v_r.reshape(B, T, H_RET, DV_RET)
    q_a = q_a.reshape(B, T, H_DIFF, 2, DK_DIFF) * (DK_DIFF ** -0.5)
    k_a = k_a.reshape(B, T, H_DIFF, 2, DK_DIFF)
    v_a = v_a.reshape(B, T, H_DIFF, DV_DIFF)
    return q_r, k_r, v_r, z_r, q_a, k_a, v_a, z_a, g_r, g_a


def _log_gamma():
    return jnp.log(1.0 - jnp.exp2(-5.0 - jnp.arange(H_RET, dtype=jnp.float32)))


def _retention_chunk(S, q, k, v, log_gamma):
    C = q.shape[2]
    i = jnp.arange(C, dtype=jnp.float32)
    diff = i[:, None] - i[None, :]
    decay = jnp.where(diff >= 0, jnp.exp(log_gamma[:, None, None] * jnp.maximum(diff, 0.0)), 0.0)
    intra = jnp.einsum('bhid,bhjd->bhij', q, k) * decay
    cross_scale = jnp.exp(log_gamma[:, None] * (i + 1.0))[:, :, None]
    out = jnp.einsum('bhij,bhjv->bhiv', intra, v) + jnp.einsum('bhid,bhdv->bhiv', q, S) * cross_scale
    k_dec = k * jnp.exp(log_gamma[:, None] * (C - 1.0 - i))[:, :, None]
    S_new = S * jnp.exp(log_gamma * C)[:, None, None] + jnp.einsum('bhjd,bhjv->bhdv', k_dec, v)
    return out, S_new


def _retention_prompt(q, k, v, log_gamma):
    B, T = q.shape[0], q.shape[1]
    nC = T // RET_CHUNK

    def to_chunks(a):
        return a.reshape(B, nC, RET_CHUNK, a.shape[2], a.shape[3]).transpose(1, 0, 3, 2, 4).astype(jnp.float32)

    S0 = jnp.zeros((B, H_RET, DK_RET, DV_RET), jnp.float32)

    def step(S, qkv):
        qc, kc, vc = qkv
        o, S = _retention_chunk(S, qc, kc, vc, log_gamma)
        return S, o

    S_fin, o = lax.scan(step, S0, (to_chunks(q), to_chunks(k), to_chunks(v)))
    o = o.transpose(1, 0, 3, 2, 4).reshape(B, T, H_RET, DV_RET)
    return o, S_fin


def _retention_sample(S, q, k, v, log_gamma):
    tr = lambda a: a.transpose(0, 2, 1, 3).astype(jnp.float32)
    o, S_new = _retention_chunk(S.astype(jnp.float32), tr(q), tr(k), tr(v), log_gamma)
    return o.transpose(0, 2, 1, 3), S_new


def _diff_lambda(lq1, lk1, lq2, lk2, lam_init):
    f = jnp.float32
    return (jnp.exp(jnp.sum(lq1.astype(f) * lk1.astype(f))) -
            jnp.exp(jnp.sum(lq2.astype(f) * lk2.astype(f))) + lam_init)


def _diff_attend(q, k, v, mask, lam):
    s = jnp.einsum('bqhcd,bkhcd->bhcqk', q, k, preferred_element_type=jnp.float32)
    s = jnp.where(mask, s, NEG_INF)
    p = jax.nn.softmax(s, axis=-1)
    a = p[:, :, 0] - lam * p[:, :, 1]
    return jnp.einsum('bhqk,bkhv->bqhv', a.astype(v.dtype), v, preferred_element_type=jnp.float32)


def _diff_prompt(q, k, v, lam):
    B, T = q.shape[0], q.shape[1]
    nQB = T // Q_BLOCK
    kpos = jnp.arange(T)

    def block(bi):
        qb = lax.dynamic_slice_in_dim(q, bi * Q_BLOCK, Q_BLOCK, axis=1)
        qpos = bi * Q_BLOCK + jnp.arange(Q_BLOCK)
        mask = kpos[None, :] <= qpos[:, None]
        return _diff_attend(qb, k, v, mask, lam)

    o = lax.map(block, jnp.arange(nQB))
    return o.transpose(1, 0, 2, 3, 4).reshape(B, T, H_DIFF, DV_DIFF)


def _diff_sample(q, k_new, v_new, cache_k_l, cache_v_l, page_table, lam):
    DB, Ts = q.shape[0], q.shape[1]
    n_pages = page_table.shape[1]
    past_len = n_pages * cache_k_l.shape[1]
    k_past = cache_k_l[page_table].reshape(DB, past_len, H_DIFF, 2, DK_DIFF).astype(k_new.dtype)
    v_past = cache_v_l[page_table].reshape(DB, past_len, H_DIFF, DV_DIFF).astype(v_new.dtype)
    k_all = jnp.concatenate([k_past, k_new], axis=1)
    v_all = jnp.concatenate([v_past, v_new], axis=1)
    mask = jnp.arange(past_len + Ts)[None, :] <= (past_len + jnp.arange(Ts))[:, None]
    return _diff_attend(q, k_all, v_all, mask, lam)


def _groupnorm(o):
    mu = jnp.mean(o, axis=-1, keepdims=True)
    var = jnp.mean(jnp.square(o - mu), axis=-1, keepdims=True)
    return (o - mu) * lax.rsqrt(var + GN_EPS)


def _subln(o, gain, lam_init):
    o = o * lax.rsqrt(jnp.mean(jnp.square(o), axis=-1, keepdims=True) + GN_EPS)
    return o * gain.astype(jnp.float32) * (1.0 - lam_init)


def _merge(x, o_r, o_a, z_r, z_a, g_r, g_a, w_br_r, w_br_a, w_out, subln_gain, lam_init, ln_gain, ln_bias):
    B, T = x.shape[0], x.shape[1]
    y_r = _groupnorm(o_r).reshape(B, T, W_RET).astype(x.dtype) * jax.nn.silu(z_r)
    y_a = _subln(o_a, subln_gain, lam_init).reshape(B, T, W_DIFF).astype(x.dtype) * jax.nn.silu(z_a)
    m = jax.nn.sigmoid(g_r) * (y_r @ w_br_r) + jax.nn.sigmoid(g_a) * (y_a @ w_br_a)
    h = (DEEPNORM_ALPHA * x + m @ w_out).astype(jnp.float32)
    mu = jnp.mean(h, axis=-1, keepdims=True)
    var = jnp.mean(jnp.square(h - mu), axis=-1, keepdims=True)
    y = (h - mu) * lax.rsqrt(var + LN_EPS) * ln_gain.astype(jnp.float32) + ln_bias.astype(jnp.float32)
    return y.astype(x.dtype)


def setup_inputs(seed: int = 0) -> dict:
    key = jax.random.key(seed)
    ks = jax.random.split(key, 20)
    n_pages = PAST_LEN // PAGE_SIZE
    n_used = DEC_BATCH * n_pages
    n_pool = (n_used * 5) // 4
    f = jnp.float32
    nrm = lambda k, shape: jax.random.normal(k, shape, f)
    page_table = jax.random.permutation(ks[5], n_pool)[:n_used].reshape(DEC_BATCH, n_pages).astype(jnp.int32)
    return {
        "x_prompt": nrm(ks[0], (BATCH, SEQ, D_MODEL)),
        "x_sample": nrm(ks[1], (DEC_BATCH, DEC_SEQ, D_MODEL)),
        "state_ret": 0.5 * nrm(ks[2], (DEPTH, DEC_BATCH, H_RET, DK_RET, DV_RET)),
        "cache_k": nrm(ks[3], (DEPTH, n_pool, PAGE_SIZE, H_DIFF, 2 * DK_DIFF)),
        "cache_v": nrm(ks[4], (DEPTH, n_pool, PAGE_SIZE, H_DIFF, DV_DIFF)),
        "page_table": page_table,
        "w_in": nrm(ks[6], (DEPTH, D_MODEL, D_IN)) * D_MODEL ** -0.5,
        "w_branch_ret": nrm(ks[7], (DEPTH, W_RET, D_MODEL)) * (W_RET ** -0.5 * DEEPNORM_BETA),
        "w_branch_diff": nrm(ks[8], (DEPTH, W_DIFF, D_MODEL)) * (W_DIFF ** -0.5 * DEEPNORM_BETA),
        "w_out": nrm(ks[9], (DEPTH, D_MODEL, D_MODEL)) * (D_MODEL ** -0.5 * DEEPNORM_BETA),
        "lambda_q1": 0.1 * nrm(ks[10], (DEPTH, DK_DIFF)),
        "lambda_k1": 0.1 * nrm(ks[11], (DEPTH, DK_DIFF)),
        "lambda_q2": 0.1 * nrm(ks[12], (DEPTH, DK_DIFF)),
        "lambda_k2": 0.1 * nrm(ks[13], (DEPTH, DK_DIFF)),
        "subln_gain": 1.0 + 0.01 * nrm(ks[14], (DEPTH, DV_DIFF)),
        "ln_gain": 1.0 + 0.01 * nrm(ks[15], (DEPTH, D_MODEL)),
        "ln_bias": 0.01 * nrm(ks[16], (DEPTH, D_MODEL)),
    }


def reference(x_prompt, x_sample, state_ret, cache_k, cache_v, page_table, w_in, w_branch_ret,
              w_branch_diff, w_out, lambda_q1, lambda_k1, lambda_q2, lambda_k2, subln_gain,
              ln_gain, ln_bias):
    log_gamma = _log_gamma()
    T_p = x_prompt.shape[1]
    T_s = x_sample.shape[1]
    past_len = page_table.shape[1] * cache_k.shape[2]
    pos_p = jnp.arange(T_p, dtype=jnp.float32)
    pos_s = past_len + jnp.arange(T_s, dtype=jnp.float32)
    hp, hs = x_prompt, x_sample
    sp_l, kp_l, vp_l, ss_l, ks_l, vs_l = [], [], [], [], [], []
    for l in range(DEPTH):
        lam_init = 0.8 - 0.6 * math.exp(-0.3 * l)
        lam = _diff_lambda(lambda_q1[l], lambda_k1[l], lambda_q2[l], lambda_k2[l], lam_init)
        q_r, k_r, v_r, z_r, q_a, k_a, v_a, z_a, g_r, g_a = _in_proj(hp, w_in[l], pos_p)
        o_r, S_p = _retention_prompt(q_r, k_r, v_r, log_gamma)
        o_a = _diff_prompt(q_a, k_a, v_a, lam)
        sp_l.append(S_p.astype(hp.dtype))
        kp_l.append(k_a.reshape(hp.shape[0], T_p, H_DIFF, 2 * DK_DIFF))
        vp_l.append(v_a)
        hp = _merge(hp, o_r, o_a, z_r, z_a, g_r, g_a, w_branch_ret[l], w_branch_diff[l], w_out[l],
                    subln_gain[l], lam_init, ln_gain[l], ln_bias[l])
        q_r, k_r, v_r, z_r, q_a, k_a, v_a, z_a, g_r, g_a = _in_proj(hs, w_in[l], pos_s)
        o_r, S_s = _retention_sample(state_ret[l], q_r, k_r, v_r, log_gamma)
        o_a = _diff_sample(q_a, k_a, v_a, cache_k[l], cache_v[l], page_table, lam)
        ss_l.append(S_s.astype(hs.dtype))
        ks_l.append(k_a.reshape(hs.shape[0], T_s, H_DIFF, 2 * DK_DIFF))
        vs_l.append(v_a)
        hs = _merge(hs, o_r, o_a, z_r, z_a, g_r, g_a, w_branch_ret[l], w_branch_diff[l], w_out[l],
                    subln_gain[l], lam_init, ln_gain[l], ln_bias[l])
    return (hp, hs, jnp.stack(sp_l), jnp.stack(kp_l), jnp.stack(vp_l),
            jnp.stack(ss_l), jnp.stack(ks_l), jnp.stack(vs_l))
```

```python
import functools
import math

import jax
import jax.numpy as jnp
import numpy as np
from jax import lax
from jax.experimental import pallas as pl
from jax.experimental.pallas import tpu as pltpu

F32 = jnp.float32
BF16 = jnp.bfloat16

H_RET = 4
DK_RET = 256
DV_RET = 256
H_DIFF = 4
DK_DIFF = 128
DV_DIFF = 256
ROPE_BASE = 10000.0
LN_EPS = 1e-5
GN_EPS = 1e-6
NEG_INF = -1e30

SEG = 1024
W_RET = H_RET * DV_RET
W_DIFF = H_DIFF * DV_DIFF
SEG_QR, SEG_KR, SEG_VR, SEG_ZR, SEG_QA, SEG_KA, SEG_VA, SEG_ZA = range(8)
SEG_GR = 8
SEG_GA = 10
N_SEG = 12

VMEM_LIMIT_BYTES = 56 * 1024 * 1024

RET_CHUNK = 256
Q_BLOCK = 512
PROJ_ROWS = 512
MERGE_ROWS = 256
SAMPLE_GROUP = 8


def _params(*sem):
    return pltpu.CompilerParams(dimension_semantics=sem, vmem_limit_bytes=VMEM_LIMIT_BYTES)


def _silu(z):
    return z * (1.0 / (1.0 + jnp.exp(-z)))


def _sigmoid(z):
    return 1.0 / (1.0 + jnp.exp(-z))


def _nt_dot(a, b):
    return lax.dot_general(a, b, (((1,), (1,)), ((), ())), preferred_element_type=F32)


def _in_proj_kernel(x_ref, w_ref, cos_ref, sin_ref, h_ref, k_ref, v_ref, xb_ref):
    j = pl.program_id(1)

    @pl.when(j == 0)
    def _cast_rows():
        xb_ref[...] = x_ref[...].astype(BF16)

    acc = jnp.dot(xb_ref[...], w_ref[...], preferred_element_type=F32)

    @pl.when(j <= SEG_KR)
    def _rope():
        scale = jnp.where(j == SEG_KR, DK_RET ** -0.5, 1.0).astype(F32)
        cos = cos_ref[...] * scale
        sin = sin_ref[...] * scale
        half = DK_RET // 2
        for hh in range(H_RET):
            a = hh * DK_RET
            x1 = acc[:, a:a + half]
            x2 = acc[:, a + half:a + DK_RET]
            h_ref[:, a:a + half] = (x1 * cos - x2 * sin).astype(h_ref.dtype)
            h_ref[:, a + half:a + DK_RET] = (x2 * cos + x1 * sin).astype(h_ref.dtype)

    @pl.when(j == SEG_QA)
    def _scaled():
        h_ref[...] = (acc * (DK_DIFF ** -0.5)).astype(h_ref.dtype)

    @pl.when(j == SEG_KA)
    def _keys():
        h_ref[...] = acc.astype(h_ref.dtype)
        k_ref[...] = acc

    @pl.when(j == SEG_VA)
    def _values():
        h_ref[...] = acc.astype(h_ref.dtype)
        v_ref[...] = acc

    @pl.when((j == SEG_VR) | (j == SEG_ZR) | (j >= SEG_ZA))
    def _plain():
        h_ref[...] = acc.astype(h_ref.dtype)


def _in_proj(x2d, w_bf, cos, sin, rows, h_dtype):
    m, d = x2d.shape
    n = w_bf.shape[1]
    assert n == N_SEG * SEG and m % rows == 0 and cos.shape[0] % rows == 0
    pos_blocks = cos.shape[0] // rows
    half = DK_RET // 2
    return pl.pallas_call(
        _in_proj_kernel,
        grid=(m // rows, N_SEG),
        in_specs=[
            pl.BlockSpec((rows, d), lambda i, j: (i, 0)),
            pl.BlockSpec((d, SEG), lambda i, j: (0, j)),
            pl.BlockSpec((rows, half), lambda i, j: (i % pos_blocks, 0)),
            pl.BlockSpec((rows, half), lambda i, j: (i % pos_blocks, 0)),
        ],
        out_specs=[
            pl.BlockSpec((rows, SEG), lambda i, j: (i, j)),
            pl.BlockSpec((rows, SEG), lambda i, j: (i, 0)),
            pl.BlockSpec((rows, SEG), lambda i, j: (i, 0)),
        ],
        out_shape=[
            jax.ShapeDtypeStruct((m, n), h_dtype),
            jax.ShapeDtypeStruct((m, SEG), F32),
            jax.ShapeDtypeStruct((m, SEG), F32),
        ],
        scratch_shapes=[pltpu.VMEM((rows, d), BF16)],
        compiler_params=_params("arbitrary", "arbitrary"),
        name="in_proj",
    )(x2d, w_bf, cos, sin)


def _retention_prompt_kernel(lg_ref, q_ref, k_ref, v_ref, z_ref, y_ref, s_out_ref,
                             s_ref, decay_ref, cross_ref, kdec_ref):
    c = pl.program_id(1)
    chunk = q_ref.shape[0]

    @pl.when(c == 0)
    def _init():
        s_ref[...] = jnp.zeros_like(s_ref)
        row = lax.broadcasted_iota(jnp.int32, (chunk, chunk), 0)
        col = lax.broadcasted_iota(jnp.int32, (chunk, chunk), 1)
        diff = (row - col).astype(F32)
        pos = lax.broadcasted_iota(jnp.int32, (chunk, DV_RET), 0).astype(F32)
        for hh in range(H_RET):
            lg = lg_ref[hh]
            decay_ref[hh] = jnp.where(diff >= 0, jnp.exp(lg * jnp.maximum(diff, 0.0)), 0.0)
            cross_ref[hh] = jnp.exp(lg * (pos + 1.0))
            kdec_ref[hh] = jnp.exp(lg * (chunk - 1.0 - pos))

    for hh in range(H_RET):
        sl = slice(hh * DK_RET, (hh + 1) * DK_RET)
        q = q_ref[:, sl]
        k = k_ref[:, sl]
        v = v_ref[:, sl]
        s_old = s_ref[hh]
        intra = (_nt_dot(q, k) * decay_ref[hh]).astype(BF16)
        o = jnp.dot(intra, v, preferred_element_type=F32)
        o = o + jnp.dot(q, s_old.astype(BF16), preferred_element_type=F32) * cross_ref[hh]
        k_dec = (k.astype(F32) * kdec_ref[hh]).astype(BF16)
        state_decay = jnp.exp(jnp.full((1, DV_RET), lg_ref[hh] * chunk, F32))
        s_new = s_old * state_decay + lax.dot_general(
            k_dec, v, (((0,), (0,)), ((), ())), preferred_element_type=F32)
        s_ref[hh] = s_new
        mu = jnp.mean(o, axis=-1, keepdims=True)
        cen = o - mu
        var = jnp.mean(cen * cen, axis=-1, keepdims=True)
        gate = _silu(z_ref[:, sl].astype(F32))
        y_ref[:, sl] = (cen * lax.rsqrt(var + GN_EPS) * gate).astype(y_ref.dtype)

    @pl.when(c == pl.num_programs(1) - 1)
    def _emit_state():
        s_out_ref[...] = s_ref[...]


def _retention_prompt(h, log_gamma, batch, seq):
    chunk = RET_CHUNK
    assert seq % chunk == 0
    nc = seq // chunk
    blk = lambda seg: pl.BlockSpec((chunk, SEG), lambda b, c, seg=seg: (b * nc + c, seg))
    return pl.pallas_call(
        _retention_prompt_kernel,
        grid=(batch, nc),
        in_specs=[pl.BlockSpec(memory_space=pltpu.SMEM),
                  blk(SEG_QR), blk(SEG_KR), blk(SEG_VR), blk(SEG_ZR)],
        out_specs=[
            pl.BlockSpec((chunk, W_RET), lambda b, c: (b * nc + c, 0)),
            pl.BlockSpec((None, H_RET, DK_RET, DV_RET), lambda b, c: (b, 0, 0, 0)),
        ],
        out_shape=[
            jax.ShapeDtypeStruct((batch * seq, W_RET), BF16),
            jax.ShapeDtypeStruct((batch, H_RET, DK_RET, DV_RET), F32),
        ],
        scratch_shapes=[
            pltpu.VMEM((H_RET, DK_RET, DV_RET), F32),
            pltpu.VMEM((H_RET, chunk, chunk), F32),
            pltpu.VMEM((H_RET, chunk, DV_RET), F32),
            pltpu.VMEM((H_RET, chunk, DK_RET), F32),
        ],
        compiler_params=_params("arbitrary", "arbitrary"),
        name="retention_prompt",
    )(log_gamma, h, h, h, h)


def _diff_lambda_in_kernel(lq1_ref, lk1_ref, lq2_ref, lk2_ref, lam_init):
    a = jnp.sum(lq1_ref[...] * lk1_ref[...], axis=-1, keepdims=True)
    b = jnp.sum(lq2_ref[...] * lk2_ref[...], axis=-1, keepdims=True)
    return jnp.exp(a) - jnp.exp(b) + lam_init


def _diff_prompt_kernel(lq1_ref, lk1_ref, lq2_ref, lk2_ref, gain_ref, q_ref, k_ref, v_ref, z_ref, y_ref,
                        m_ref, l_ref, acc_ref, *, lam_init):
    i = pl.program_id(2)
    blk = q_ref.shape[0]
    qs = (q_ref[:, :DK_DIFF], q_ref[:, DK_DIFF:])

    m_ref[...] = jnp.full_like(m_ref, NEG_INF)
    l_ref[...] = jnp.zeros_like(l_ref)
    acc_ref[...] = jnp.zeros_like(acc_ref)

    def update(j, mask):
        start = pl.multiple_of(j * blk, blk)
        kb = k_ref[pl.ds(start, blk), :]
        vb = v_ref[pl.ds(start, blk), :]
        for c in range(2):
            s = _nt_dot(qs[c], kb[:, c * DK_DIFF:(c + 1) * DK_DIFF])
            if mask is not None:
                s = jnp.where(mask, s, NEG_INF)
            m_old = m_ref[c]
            m_new = jnp.maximum(m_old, jnp.max(s, axis=-1, keepdims=True))
            alpha = jnp.exp(m_old - m_new)
            p = jnp.exp(s - m_new)
            l_ref[c] = alpha * l_ref[c] + jnp.sum(p, axis=-1, keepdims=True)
            acc_ref[c] = alpha * acc_ref[c] + jnp.dot(p.astype(BF16), vb, preferred_element_type=F32)
            m_ref[c] = m_new

    def body(j, carry):
        update(j, None)
        return carry

    lax.fori_loop(0, i, body, 0)
    row = lax.broadcasted_iota(jnp.int32, (blk, blk), 0)
    col = lax.broadcasted_iota(jnp.int32, (blk, blk), 1)
    update(i, col <= row)

    lam = _diff_lambda_in_kernel(lq1_ref, lk1_ref, lq2_ref, lk2_ref, lam_init)
    o = acc_ref[0] / l_ref[0] - lam * (acc_ref[1] / l_ref[1])
    o = o * lax.rsqrt(jnp.mean(o * o, axis=-1, keepdims=True) + GN_EPS)
    o = o * gain_ref[...] * (1.0 - lam_init)
    y_ref[...] = (o * _silu(z_ref[...].astype(F32))).astype(y_ref.dtype)


def _diff_prompt(h, lam_params, gain, lam_init, batch, seq):
    blk = Q_BLOCK
    assert seq % blk == 0
    nq = seq // blk
    heads_per_seg = SEG // DV_DIFF
    small = pl.BlockSpec((1, DK_DIFF), lambda b, hh, i: (0, 0))
    return pl.pallas_call(
        functools.partial(_diff_prompt_kernel, lam_init=lam_init),
        grid=(batch, H_DIFF, nq),
        in_specs=[
            small, small, small, small,
            pl.BlockSpec((1, DV_DIFF), lambda b, hh, i: (0, 0)),
            pl.BlockSpec((blk, 2 * DK_DIFF), lambda b, hh, i: (b * nq + i, SEG_QA * heads_per_seg + hh)),
            pl.BlockSpec((seq, 2 * DK_DIFF), lambda b, hh, i: (b, SEG_KA * heads_per_seg + hh)),
            pl.BlockSpec((seq, DV_DIFF), lambda b, hh, i: (b, SEG_VA * heads_per_seg + hh)),
            pl.BlockSpec((blk, DV_DIFF), lambda b, hh, i: (b * nq + i, SEG_ZA * heads_per_seg + hh)),
        ],
        out_specs=pl.BlockSpec((blk, DV_DIFF), lambda b, hh, i: (b * nq + i, hh)),
        out_shape=jax.ShapeDtypeStruct((batch * seq, W_DIFF), BF16),
        scratch_shapes=[
            pltpu.VMEM((2, blk, 1), F32),
            pltpu.VMEM((2, blk, 1), F32),
            pltpu.VMEM((2, blk, DV_DIFF), F32),
        ],
        compiler_params=_params("arbitrary", "arbitrary", "arbitrary"),
        name="diff_prompt",
    )(*lam_params, gain, h, h, h, h)


def _retention_sample_kernel(lg_ref, q_ref, k_ref, kt_ref, v_ref, vall_ref, z_ref, s_in_ref,
                             y_ref, s_out_ref):
    g = pl.program_id(0)
    hh = pl.program_id(1)
    group = q_ref.shape[0]
    n_seq = kt_ref.shape[1]
    gamma = jnp.exp(jnp.full((1, DV_RET), lg_ref[hh], F32))
    q_bf = q_ref[...].astype(BF16)
    k_bf = k_ref[...].astype(BF16)
    v_bf = v_ref[...].astype(BF16)
    qk = jnp.sum(q_bf.astype(F32) * k_bf.astype(F32), axis=-1, keepdims=True)
    kt_bf = kt_ref[...].astype(BF16)
    vall_bf = vall_ref[...].astype(BF16)
    seq_lane = lax.broadcasted_iota(jnp.int32, (DK_RET, n_seq), 1)
    row = lax.broadcasted_iota(jnp.int32, (group, DV_RET), 0)
    cross = jnp.zeros((group, DV_RET), F32)
    for s in range(group):
        s_old = s_in_ref[s]
        qs = jnp.dot(q_bf, s_old.astype(BF16), preferred_element_type=F32)
        cross = jnp.where(row == s, qs, cross)
        k_only = jnp.where(seq_lane == g * group + s, kt_bf, jnp.zeros_like(kt_bf))
        outer = jnp.dot(k_only, vall_bf, preferred_element_type=F32)
        s_out_ref[s] = s_old * gamma + outer
    o = qk * v_bf.astype(F32) + cross * gamma
    mu = jnp.mean(o, axis=-1, keepdims=True)
    cen = o - mu
    var = jnp.mean(cen * cen, axis=-1, keepdims=True)
    y_ref[...] = (cen * lax.rsqrt(var + GN_EPS) * _silu(z_ref[...].astype(F32))).astype(y_ref.dtype)


def _retention_sample(hs, kt, state, log_gamma):
    n_seq = hs.shape[0]
    group = SAMPLE_GROUP
    assert n_seq % group == 0
    per_seg = SEG // DK_RET
    col = lambda seg: pl.BlockSpec((group, DK_RET), lambda g, hh, seg=seg: (g, seg * per_seg + hh))
    st = pl.BlockSpec((group, None, DK_RET, DV_RET), lambda g, hh: (g, hh, 0, 0))
    return pl.pallas_call(
        _retention_sample_kernel,
        grid=(n_seq // group, H_RET),
        in_specs=[
            pl.BlockSpec(memory_space=pltpu.SMEM),
            col(SEG_QR), col(SEG_KR),
            pl.BlockSpec((DK_RET, n_seq), lambda g, hh: (hh, 0)),
            col(SEG_VR),
            pl.BlockSpec((n_seq, DV_RET), lambda g, hh: (0, SEG_VR * per_seg + hh)),
            col(SEG_ZR),
            st,
        ],
        out_specs=[pl.BlockSpec((group, DV_RET), lambda g, hh: (g, hh)), st],
        out_shape=[
            jax.ShapeDtypeStruct((n_seq, W_RET), F32),
            jax.ShapeDtypeStruct(state.shape, F32),
        ],
        compiler_params=_params("arbitrary", "arbitrary"),
        name="retention_sample",
    )(log_gamma, hs, hs, kt, hs, hs, hs, state)


def _diff_sample_kernel(pt_ref, lq1_ref, lk1_ref, lq2_ref, lk2_ref, gain_ref, q_ref, kn_ref, vn_ref, z_ref,
                        *rest, n_pages, lam_init):
    k_pages = rest[:n_pages]
    v_pages = rest[n_pages:2 * n_pages]
    y_ref = rest[2 * n_pages]
    s_ref, bias_ref = rest[2 * n_pages + 1:]
    page = k_pages[0].shape[0]
    rows = page * H_DIFF
    lanes = bias_ref.shape[-1]

    @pl.when(pl.program_id(0) == 0)
    def _init_bias():
        r = lax.broadcasted_iota(jnp.int32, (rows, lanes), 0)
        j = lax.broadcasted_iota(jnp.int32, (rows, lanes), 1)
        bias_ref[...] = jnp.where((j < 2 * H_DIFF) & (r % H_DIFF == j % H_DIFF), 0.0, NEG_INF)

    q4 = q_ref[...]
    in_map0 = lax.broadcasted_iota(jnp.int32, q4.shape, 1) < DK_DIFF
    q_cols = jnp.concatenate(
        [jnp.where(in_map0, q4, 0.0), jnp.where(in_map0, 0.0, q4),
         jnp.zeros((lanes - 2 * H_DIFF, 2 * DK_DIFF), F32)], axis=0).astype(BF16)

    kn2 = jnp.concatenate([kn_ref[...], kn_ref[...]], axis=0)
    new_rows = kn2.shape[0]
    new_bias = jnp.where(lax.broadcasted_iota(jnp.int32, (new_rows, lanes), 0) < H_DIFF,
                         bias_ref[0:new_rows, :], NEG_INF)
    s_new = _nt_dot(kn2.astype(BF16), q_cols) + new_bias
    m = jnp.max(s_new, axis=0, keepdims=True)
    for p in range(n_pages):
        k_mat = k_pages[p][...].reshape(rows, 2 * DK_DIFF).astype(BF16)
        s = _nt_dot(k_mat, q_cols) + bias_ref[...]
        s_ref[p] = s
        m = jnp.maximum(m, jnp.max(s, axis=0, keepdims=True))
    e_new = jnp.exp(s_new - m)
    l = jnp.sum(e_new, axis=0, keepdims=True)
    for p in range(n_pages):
        e = jnp.exp(s_ref[p] - m)
        s_ref[p] = e
        l = l + jnp.sum(e, axis=0, keepdims=True)
    lam = _diff_lambda_in_kernel(lq1_ref, lk1_ref, lq2_ref, lk2_ref, lam_init)
    lane = lax.broadcasted_iota(jnp.int32, (1, lanes), 1)
    coef = jnp.where(lane < H_DIFF, 1.0, -lam) / l
    coef = jnp.where(lane < 2 * H_DIFF, coef, 0.0)
    spread = jnp.ones((lanes, DV_DIFF), BF16)
    a_new = jnp.dot((e_new * coef).astype(BF16), spread, preferred_element_type=F32)
    acc = a_new * jnp.concatenate([vn_ref[...], vn_ref[...]], axis=0)
    for p in range(n_pages):
        a = jnp.dot((s_ref[p] * coef).astype(BF16), spread, preferred_element_type=F32)
        av = a * v_pages[p][...].reshape(rows, DV_DIFF)
        acc = acc + jnp.sum(av.reshape(rows // new_rows, new_rows, DV_DIFF), axis=0)
    o = acc[0:H_DIFF, :] + acc[H_DIFF:new_rows, :]
    o = o * lax.rsqrt(jnp.mean(o * o, axis=-1, keepdims=True) + GN_EPS)
    o = o * gain_ref[...] * (1.0 - lam_init)
    y_ref[...] = (o * _silu(z_ref[...])).astype(y_ref.dtype)


def _diff_sample(q4, kn4, vn4, z4, cache_k, cache_v, layer, page_table, lam_params, gain, lam_init):
    n_seq = q4.shape[0]
    n_pages = page_table.shape[1]
    page = cache_k.shape[2]
    lanes = 128
    small = pl.BlockSpec((1, DK_DIFF), lambda b, pt: (0, 0))
    tok = pl.BlockSpec((None, H_DIFF, DV_DIFF), lambda b, pt: (b, 0, 0))
    pg = lambda p: pl.BlockSpec((None, None, page, H_DIFF, DV_DIFF),
                                lambda b, pt, p=p: (layer, pt[b, p], 0, 0, 0))
    grid_spec = pltpu.PrefetchScalarGridSpec(
        num_scalar_prefetch=1,
        grid=(n_seq,),
        in_specs=[small, small, small, small,
                  pl.BlockSpec((1, DV_DIFF), lambda b, pt: (0, 0)),
                  tok, tok, tok, tok]
                 + [pg(p) for p in range(n_pages)] + [pg(p) for p in range(n_pages)],
        out_specs=tok,
        scratch_shapes=[pltpu.VMEM((n_pages, page * H_DIFF, lanes), F32),
                        pltpu.VMEM((page * H_DIFF, lanes), F32)],
    )
    return pl.pallas_call(
        functools.partial(_diff_sample_kernel, n_pages=n_pages, lam_init=lam_init),
        grid_spec=grid_spec,
        out_shape=jax.ShapeDtypeStruct((n_seq, H_DIFF, DV_DIFF), F32),
        compiler_params=_params("arbitrary"),
        name="diff_sample",
    )(page_table, *lam_params, gain, q4, kn4, vn4, z4,
      *([cache_k] * n_pages), *([cache_v] * n_pages))


def _merge_kernel(x_ref, yr_ref, ya_ref, gr_ref, ga_ref, wr_ref, wa_ref, wo_ref, lng_ref, lnb_ref, y_ref,
                  *, alpha):
    br = jnp.dot(yr_ref[...].astype(BF16), wr_ref[...], preferred_element_type=F32)
    ba = jnp.dot(ya_ref[...].astype(BF16), wa_ref[...], preferred_element_type=F32)
    m = _sigmoid(gr_ref[...].astype(F32)) * br + _sigmoid(ga_ref[...].astype(F32)) * ba
    h = alpha * x_ref[...] + jnp.dot(m.astype(BF16), wo_ref[...], preferred_element_type=F32)
    mu = jnp.mean(h, axis=-1, keepdims=True)
    cen = h - mu
    var = jnp.mean(cen * cen, axis=-1, keepdims=True)
    y_ref[...] = cen * lax.rsqrt(var + LN_EPS) * lng_ref[...] + lnb_ref[...]


def _merge(x2d, y_r, y_a, h, w_r, w_a, w_o, ln_gain, ln_bias, rows, alpha):
    m, d = x2d.shape
    assert m % rows == 0
    gate_blocks = (2 * SEG) // d if d <= 2 * SEG else None
    assert gate_blocks == 1, "gates are two segments wide, same as the model width"
    const = lambda shape: pl.BlockSpec(shape, lambda i: (0, 0), pipeline_mode=pl.Buffered(1))
    return pl.pallas_call(
        functools.partial(_merge_kernel, alpha=alpha),
        grid=(m // rows,),
        in_specs=[
            pl.BlockSpec((rows, d), lambda i: (i, 0)),
            pl.BlockSpec((rows, W_RET), lambda i: (i, 0)),
            pl.BlockSpec((rows, W_DIFF), lambda i: (i, 0)),
            pl.BlockSpec((rows, d), lambda i: (i, SEG_GR * SEG // d)),
            pl.BlockSpec((rows, d), lambda i: (i, SEG_GA * SEG // d)),
            const((W_RET, d)), const((W_DIFF, d)), const((d, d)),
            const((1, d)), const((1, d)),
        ],
        out_specs=pl.BlockSpec((rows, d), lambda i: (i, 0)),
        out_shape=jax.ShapeDtypeStruct((m, d), F32),
        compiler_params=_params("arbitrary"),
        name="merge",
    )(x2d, y_r, y_a, h, h, w_r, w_a, w_o, ln_gain, ln_bias)


def _rope_tables(pos):
    half = DK_RET // 2
    inv_freq = 1.0 / (ROPE_BASE ** (jnp.arange(half, dtype=F32) / half))
    ang = pos[:, None] * inv_freq[None, :]
    return jnp.cos(ang), jnp.sin(ang)


def kernel(x_prompt, x_sample, state_ret, cache_k, cache_v, page_table, w_in, w_branch_ret, w_branch_diff, w_out, lambda_q1, lambda_k1, lambda_q2, lambda_k2, subln_gain, ln_gain, ln_bias):
    batch, seq, d_model = x_prompt.shape
    n_seq, t_s, _ = x_sample.shape
    depth = w_in.shape[0]
    assert t_s == 1, "the sample kernels implement the one-new-token step"
    n_pool, page = cache_k.shape[1], cache_k.shape[2]
    past_len = page_table.shape[1] * page
    alpha = (2.0 * depth) ** 0.25

    log_gamma = jnp.log(1.0 - jnp.exp2(-5.0 - jnp.arange(H_RET, dtype=F32)))
    cos_p, sin_p = _rope_tables(jnp.arange(seq, dtype=F32))
    cos_s, sin_s = _rope_tables(jnp.full((n_seq,), past_len, F32))

    hp = x_prompt.reshape(batch * seq, d_model)
    hs = x_sample.reshape(n_seq, d_model)
    outs = [[] for _ in range(6)]
    for l in range(depth):
        lam_init = 0.8 - 0.6 * math.exp(-0.3 * l)
        w_in_bf = w_in[l].astype(BF16)
        w_r_bf = w_branch_ret[l].astype(BF16)
        w_a_bf = w_branch_diff[l].astype(BF16)
        w_o_bf = w_out[l].astype(BF16)
        lam_params = [p[l].reshape(1, DK_DIFF) for p in (lambda_q1, lambda_k1, lambda_q2, lambda_k2)]
        gain = subln_gain[l].reshape(1, DV_DIFF)
        lng = ln_gain[l].reshape(1, d_model)
        lnb = ln_bias[l].reshape(1, d_model)

        h_p, k_p, v_p = _in_proj(hp, w_in_bf, cos_p, sin_p, PROJ_ROWS, BF16)
        y_r, s_p = _retention_prompt(h_p, log_gamma, batch, seq)
        y_a = _diff_prompt(h_p, lam_params, gain, lam_init, batch, seq)
        hp = _merge(hp, y_r, y_a, h_p, w_r_bf, w_a_bf, w_o_bf, lng, lnb, MERGE_ROWS, alpha)

        h_s, k_s, v_s = _in_proj(hs, w_in_bf, cos_s, sin_s, n_seq, F32)
        kt = h_s[:, SEG_KR * SEG:(SEG_KR + 1) * SEG].T
        y_rs, s_s = _retention_sample(h_s, kt, state_ret[l], log_gamma)
        tok = lambda seg: h_s[:, seg * SEG:(seg + 1) * SEG].reshape(n_seq, H_DIFF, DV_DIFF)
        y_as = _diff_sample(tok(SEG_QA), tok(SEG_KA), tok(SEG_VA), tok(SEG_ZA), cache_k, cache_v, l,
                            page_table, lam_params, gain, lam_init)
        hs = _merge(hs, y_rs, y_as.reshape(n_seq, W_DIFF), h_s, w_r_bf, w_a_bf, w_o_bf, lng, lnb, n_seq, alpha)

        outs[0].append(s_p)
        outs[1].append(k_p.reshape(batch, seq, H_DIFF, 2 * DK_DIFF))
        outs[2].append(v_p.reshape(batch, seq, H_DIFF, DV_DIFF))
        outs[3].append(s_s)
        outs[4].append(k_s.reshape(n_seq, t_s, H_DIFF, 2 * DK_DIFF))
        outs[5].append(v_s.reshape(n_seq, t_s, H_DIFF, DV_DIFF))

    return (hp.reshape(batch, seq, d_model), hs.reshape(n_seq, t_s, d_model),
            *[jnp.stack(o) for o in outs])
```

```python
import functools
import math

import jax
import jax.numpy as jnp
import numpy as np
from jax import lax
from jax.experimental import pallas as pl
from jax.experimental.pallas import tpu as pltpu

F32 = jnp.float32
BF16 = jnp.bfloat16

H_RET = 4
DK_RET = 256
DV_RET = 256
H_DIFF = 4
DK_DIFF = 128
DV_DIFF = 256
ROPE_BASE = 10000.0
LN_EPS = 1e-5
GN_EPS = 1e-6
NEG_INF = -1e30

SEG = 1024
W_RET = H_RET * DV_RET
W_DIFF = H_DIFF * DV_DIFF
SEG_QR, SEG_KR, SEG_VR, SEG_ZR, SEG_QA, SEG_KA, SEG_VA, SEG_ZA = range(8)
SEG_GR = 8
SEG_GA = 10
N_SEG = 12

LANES = 128
LOG2E = math.log2(math.e)
VMEM_LIMIT_BYTES = 56 * 1024 * 1024

RET_CHUNK = 256
Q_BLOCK = 512
PROJ_ROWS = 512
MERGE_ROWS = 256
SAMPLE_GROUP = 16


def _params(*sem):
    return pltpu.CompilerParams(dimension_semantics=sem, vmem_limit_bytes=VMEM_LIMIT_BYTES)


def _silu(z):
    return z * (1.0 / (1.0 + jnp.exp(-z)))


def _sigmoid(z):
    return 1.0 / (1.0 + jnp.exp(-z))


def _nt_dot(a, b):
    return lax.dot_general(a, b, (((1,), (1,)), ((), ())), preferred_element_type=F32)


def _in_proj_kernel(x_ref, w_ref, cos_ref, sin_ref, h_ref, k_ref, v_ref, xb_ref, *, q_scale):
    j = pl.program_id(1)

    @pl.when(j == 0)
    def _cast_rows():
        xb_ref[...] = x_ref[...].astype(BF16)

    acc = jnp.dot(xb_ref[...], w_ref[...], preferred_element_type=F32)

    @pl.when(j <= SEG_KR)
    def _rope():
        scale = jnp.where(j == SEG_KR, DK_RET ** -0.5, 1.0).astype(F32)
        cos = cos_ref[...] * scale
        sin = sin_ref[...] * scale
        half = DK_RET // 2
        for hh in range(H_RET):
            a = hh * DK_RET
            x1 = acc[:, a:a + half]
            x2 = acc[:, a + half:a + DK_RET]
            h_ref[:, a:a + half] = (x1 * cos - x2 * sin).astype(h_ref.dtype)
            h_ref[:, a + half:a + DK_RET] = (x2 * cos + x1 * sin).astype(h_ref.dtype)

    @pl.when(j == SEG_QA)
    def _scaled():
        h_ref[...] = (acc * q_scale).astype(h_ref.dtype)

    @pl.when(j == SEG_KA)
    def _keys():
        h_ref[...] = acc.astype(h_ref.dtype)
        k_ref[...] = acc.reshape(k_ref.shape)

    @pl.when(j == SEG_VA)
    def _values():
        h_ref[...] = acc.astype(h_ref.dtype)
        v_ref[...] = acc.reshape(v_ref.shape)

    @pl.when((j == SEG_VR) | (j == SEG_ZR) | (j >= SEG_ZA))
    def _plain():
        h_ref[...] = acc.astype(h_ref.dtype)


def _in_proj(x2d, w_bf, cos, sin, rows, h_dtype, q_scale):
    m, d = x2d.shape
    n = w_bf.shape[1]
    assert n == N_SEG * SEG and m % rows == 0 and cos.shape[0] % rows == 0
    pos_blocks = cos.shape[0] // rows
    half = DK_RET // 2
    return pl.pallas_call(
        functools.partial(_in_proj_kernel, q_scale=q_scale),
        grid=(m // rows, N_SEG),
        in_specs=[
            pl.BlockSpec((rows, d), lambda i, j: (i, 0)),
            pl.BlockSpec((d, SEG), lambda i, j: (0, j)),
            pl.BlockSpec((rows, half), lambda i, j: (i % pos_blocks, 0)),
            pl.BlockSpec((rows, half), lambda i, j: (i % pos_blocks, 0)),
        ],
        out_specs=[
            pl.BlockSpec((rows, SEG), lambda i, j: (i, j)),
            pl.BlockSpec((rows, H_DIFF, 2 * DK_DIFF), lambda i, j: (i, 0, 0)),
            pl.BlockSpec((rows, H_DIFF, DV_DIFF), lambda i, j: (i, 0, 0)),
        ],
        out_shape=[
            jax.ShapeDtypeStruct((m, n), h_dtype),
            jax.ShapeDtypeStruct((m, H_DIFF, 2 * DK_DIFF), F32),
            jax.ShapeDtypeStruct((m, H_DIFF, DV_DIFF), F32),
        ],
        scratch_shapes=[pltpu.VMEM((rows, d), BF16)],
        compiler_params=_params("arbitrary", "arbitrary"),
        name="in_proj",
    )(x2d, w_bf, cos, sin)


def _retention_prompt_kernel(lg_ref, q_ref, k_ref, v_ref, z_ref, y_ref, s_out_ref,
                             s_ref, decay_ref, cross_ref, kdec_ref):
    c = pl.program_id(1)
    chunk = q_ref.shape[0]

    @pl.when(c == 0)
    def _init():
        s_ref[...] = jnp.zeros_like(s_ref)
        row = lax.broadcasted_iota(jnp.int32, (chunk, chunk), 0)
        col = lax.broadcasted_iota(jnp.int32, (chunk, chunk), 1)
        diff = (row - col).astype(F32)
        pos = lax.broadcasted_iota(jnp.int32, (chunk, DV_RET), 0).astype(F32)
        for hh in range(H_RET):
            lg = lg_ref[hh]
            decay_ref[hh] = jnp.where(diff >= 0, jnp.exp(lg * jnp.maximum(diff, 0.0)), 0.0)
            cross_ref[hh] = jnp.exp(lg * (pos + 1.0))
            kdec_ref[hh] = jnp.exp(lg * (chunk - 1.0 - pos))

    for hh in range(H_RET):
        sl = slice(hh * DK_RET, (hh + 1) * DK_RET)
        q = q_ref[:, sl]
        k = k_ref[:, sl]
        v = v_ref[:, sl]
        s_old = s_ref[hh]
        intra = (_nt_dot(q, k) * decay_ref[hh]).astype(BF16)
        o = jnp.dot(intra, v, preferred_element_type=F32)
        o = o + jnp.dot(q, s_old.astype(BF16), preferred_element_type=F32) * cross_ref[hh]
        k_dec = (k.astype(F32) * kdec_ref[hh]).astype(BF16)
        state_decay = jnp.exp(jnp.full((1, DV_RET), lg_ref[hh] * chunk, F32))
        s_new = s_old * state_decay + lax.dot_general(
            k_dec, v, (((0,), (0,)), ((), ())), preferred_element_type=F32)
        s_ref[hh] = s_new
        mu = jnp.mean(o, axis=-1, keepdims=True)
        cen = o - mu
        var = jnp.mean(cen * cen, axis=-1, keepdims=True)
        gate = _silu(z_ref[:, sl].astype(F32))
        y_ref[:, sl] = (cen * lax.rsqrt(var + GN_EPS) * gate).astype(y_ref.dtype)

    @pl.when(c == pl.num_programs(1) - 1)
    def _emit_state():
        s_out_ref[...] = s_ref[...]


def _retention_prompt(h, log_gamma, batch, seq):
    chunk = RET_CHUNK
    assert seq % chunk == 0
    nc = seq // chunk
    blk = lambda seg: pl.BlockSpec((chunk, SEG), lambda b, c, seg=seg: (b * nc + c, seg))
    return pl.pallas_call(
        _retention_prompt_kernel,
        grid=(batch, nc),
        in_specs=[pl.BlockSpec(memory_space=pltpu.SMEM),
                  blk(SEG_QR), blk(SEG_KR), blk(SEG_VR), blk(SEG_ZR)],
        out_specs=[
            pl.BlockSpec((chunk, W_RET), lambda b, c: (b * nc + c, 0)),
            pl.BlockSpec((None, H_RET, DK_RET, DV_RET), lambda b, c: (b, 0, 0, 0)),
        ],
        out_shape=[
            jax.ShapeDtypeStruct((batch * seq, W_RET), BF16),
            jax.ShapeDtypeStruct((batch, H_RET, DK_RET, DV_RET), F32),
        ],
        scratch_shapes=[
            pltpu.VMEM((H_RET, DK_RET, DV_RET), F32),
            pltpu.VMEM((H_RET, chunk, chunk), F32),
            pltpu.VMEM((H_RET, chunk, DV_RET), F32),
            pltpu.VMEM((H_RET, chunk, DK_RET), F32),
        ],
        compiler_params=_params("arbitrary", "arbitrary"),
        name="retention_prompt",
    )(log_gamma, h, h, h, h)


def _diff_lambda_in_kernel(lq1_ref, lk1_ref, lq2_ref, lk2_ref, lam_init):
    a = jnp.sum(lq1_ref[...] * lk1_ref[...], axis=-1, keepdims=True)
    b = jnp.sum(lq2_ref[...] * lk2_ref[...], axis=-1, keepdims=True)
    return jnp.exp(a) - jnp.exp(b) + lam_init


def _diff_prompt_kernel(lq1_ref, lk1_ref, lq2_ref, lk2_ref, gain_ref, q_ref, k_ref, v_ref, z_ref, y_ref,
                        sa_ref, sb_ref, m_ref, l_ref, acc_ref, *, lam_init):
    i = pl.program_id(2)
    blk = q_ref.shape[0]
    lanes = m_ref.shape[-1]
    qs = (q_ref[:, :DK_DIFF], q_ref[:, DK_DIFF:])

    m_ref[...] = jnp.full_like(m_ref, NEG_INF)
    l_ref[...] = jnp.zeros_like(l_ref)
    acc_ref[...] = jnp.zeros_like(acc_ref)

    def scores_into(buf_ref, j):
        start = pl.multiple_of(j * blk, blk)
        kb = k_ref[pl.ds(start, blk), :]
        for c in range(2):
            buf_ref[c] = _nt_dot(qs[c], kb[:, c * DK_DIFF:(c + 1) * DK_DIFF])

    def softmax_pv(buf_ref, j, masked):
        start = pl.multiple_of(j * blk, blk)
        vb = v_ref[pl.ds(start, blk), :]
        if masked:
            row = lax.broadcasted_iota(jnp.int32, (blk, blk), 0)
            col = lax.broadcasted_iota(jnp.int32, (blk, blk), 1)
            keep = col <= row
        for c in range(2):
            s = buf_ref[c]
            if masked:
                s = jnp.where(keep, s, NEG_INF)
            m_old = m_ref[c]
            m_new = jnp.maximum(m_old, jnp.max(s, axis=-1, keepdims=True))
            alpha = jnp.exp2(m_old - m_new)
            p = jnp.exp2(s - pltpu.repeat(m_new, blk // lanes, axis=1))
            part = p[:, 0:lanes]
            for t in range(1, blk // lanes):
                part = part + p[:, t * lanes:(t + 1) * lanes]
            l_ref[c] = alpha * l_ref[c] + part
            acc_ref[c] = (pltpu.repeat(alpha, DV_DIFF // lanes, axis=1) * acc_ref[c]
                          + jnp.dot(p.astype(BF16), vb, preferred_element_type=F32))
            m_ref[c] = m_new

    scores_into(sa_ref, 0)

    def pair(t, carry):
        j = 2 * t
        scores_into(sb_ref, j + 1)
        softmax_pv(sa_ref, j, False)
        scores_into(sa_ref, j + 2)
        softmax_pv(sb_ref, j + 1, False)
        return carry

    lax.fori_loop(0, i // 2, pair, 0)

    @pl.when(i % 2 == 1)
    def _odd_tail():
        scores_into(sb_ref, i)
        softmax_pv(sa_ref, i - 1, False)
        softmax_pv(sb_ref, i, True)

    @pl.when(i % 2 == 0)
    def _even_tail():
        softmax_pv(sa_ref, i, True)

    lam = _diff_lambda_in_kernel(lq1_ref, lk1_ref, lq2_ref, lk2_ref, lam_init)
    l0 = jnp.sum(l_ref[0], axis=-1, keepdims=True)
    l1 = jnp.sum(l_ref[1], axis=-1, keepdims=True)
    o = acc_ref[0] / l0 - lam * (acc_ref[1] / l1)
    o = o * lax.rsqrt(jnp.mean(o * o, axis=-1, keepdims=True) + GN_EPS)
    o = o * gain_ref[...] * (1.0 - lam_init)
    y_ref[...] = (o * _silu(z_ref[...].astype(F32))).astype(y_ref.dtype)


def _diff_prompt(h, lam_params, gain, lam_init, batch, seq):
    blk = Q_BLOCK
    assert seq % blk == 0
    nq = seq // blk
    heads_per_seg = SEG // DV_DIFF
    small = pl.BlockSpec((1, DK_DIFF), lambda b, hh, i: (0, 0))
    return pl.pallas_call(
        functools.partial(_diff_prompt_kernel, lam_init=lam_init),
        grid=(batch, H_DIFF, nq),
        in_specs=[
            small, small, small, small,
            pl.BlockSpec((1, DV_DIFF), lambda b, hh, i: (0, 0)),
            pl.BlockSpec((blk, 2 * DK_DIFF), lambda b, hh, i: (b * nq + i, SEG_QA * heads_per_seg + hh)),
            pl.BlockSpec((seq, 2 * DK_DIFF), lambda b, hh, i: (b, SEG_KA * heads_per_seg + hh)),
            pl.BlockSpec((seq, DV_DIFF), lambda b, hh, i: (b, SEG_VA * heads_per_seg + hh)),
            pl.BlockSpec((blk, DV_DIFF), lambda b, hh, i: (b * nq + i, SEG_ZA * heads_per_seg + hh)),
        ],
        out_specs=pl.BlockSpec((blk, DV_DIFF), lambda b, hh, i: (b * nq + i, hh)),
        out_shape=jax.ShapeDtypeStruct((batch * seq, W_DIFF), BF16),
        scratch_shapes=[
            pltpu.VMEM((2, blk, blk), F32),
            pltpu.VMEM((2, blk, blk), F32),
            pltpu.VMEM((2, blk, LANES), F32),
            pltpu.VMEM((2, blk, LANES), F32),
            pltpu.VMEM((2, blk, DV_DIFF), F32),
        ],
        compiler_params=_params("arbitrary", "arbitrary", "arbitrary"),
        name="diff_prompt",
    )(*lam_params, gain, h, h, h, h)


def _retention_sample_kernel(lg_ref, q_ref, k_ref, kt_ref, v_ref, vall_ref, z_ref, s_in_ref,
                             y_ref, s_out_ref):
    g = pl.program_id(0)
    hh = pl.program_id(1)
    group = q_ref.shape[0]
    n_seq = kt_ref.shape[1]
    gamma = jnp.exp(jnp.full((1, DV_RET), lg_ref[hh], F32))
    q_bf = q_ref[...].astype(BF16)
    k_bf = k_ref[...].astype(BF16)
    v_bf = v_ref[...].astype(BF16)
    qk = jnp.sum(q_bf.astype(F32) * k_bf.astype(F32), axis=-1, keepdims=True)
    kt_bf = kt_ref[...].astype(BF16)
    vall_bf = vall_ref[...].astype(BF16)
    seq_lane = lax.broadcasted_iota(jnp.int32, (DK_RET, n_seq), 1)
    row = lax.broadcasted_iota(jnp.int32, (group, DV_RET), 0)
    cross = jnp.zeros((group, DV_RET), F32)
    for s in range(group):
        s_old = s_in_ref[s]
        qs = jnp.dot(q_bf, s_old.astype(BF16), preferred_element_type=F32)
        cross = jnp.where(row == s, qs, cross)
        k_only = jnp.where(seq_lane == g * group + s, kt_bf, jnp.zeros_like(kt_bf))
        outer = jnp.dot(k_only, vall_bf, preferred_element_type=F32)
        s_out_ref[s] = s_old * gamma + outer
    o = qk * v_bf.astype(F32) + cross * gamma
    mu = jnp.mean(o, axis=-1, keepdims=True)
    cen = o - mu
    var = jnp.mean(cen * cen, axis=-1, keepdims=True)
    y_ref[...] = (cen * lax.rsqrt(var + GN_EPS) * _silu(z_ref[...].astype(F32))).astype(y_ref.dtype)


def _retention_sample(hs, kt, state, log_gamma):
    n_seq = hs.shape[0]
    group = SAMPLE_GROUP
    assert n_seq % group == 0
    per_seg = SEG // DK_RET
    col = lambda seg: pl.BlockSpec((group, DK_RET), lambda g, hh, seg=seg: (g, seg * per_seg + hh))
    st = pl.BlockSpec((group, None, DK_RET, DV_RET), lambda g, hh: (g, hh, 0, 0))
    return pl.pallas_call(
        _retention_sample_kernel,
        grid=(n_seq // group, H_RET),
        in_specs=[
            pl.BlockSpec(memory_space=pltpu.SMEM),
            col(SEG_QR), col(SEG_KR),
            pl.BlockSpec((DK_RET, n_seq), lambda g, hh: (hh, 0)),
            col(SEG_VR),
            pl.BlockSpec((n_seq, DV_RET), lambda g, hh: (0, SEG_VR * per_seg + hh)),
            col(SEG_ZR),
            st,
        ],
        out_specs=[pl.BlockSpec((group, DV_RET), lambda g, hh: (g, hh)), st],
        out_shape=[
            jax.ShapeDtypeStruct((n_seq, W_RET), F32),
            jax.ShapeDtypeStruct(state.shape, F32),
        ],
        compiler_params=_params("arbitrary", "arbitrary"),
        name="retention_sample",
    )(log_gamma, hs, hs, kt, hs, hs, hs, state)


def _diff_sample_kernel(pt_ref, lq1_ref, lk1_ref, lq2_ref, lk2_ref, gain_ref, q_ref, kn_ref, vn_ref, z_ref,
                        *rest, n_pages, lam_init):
    k_pages = rest[:n_pages]
    v_pages = rest[n_pages:2 * n_pages]
    y_ref = rest[2 * n_pages]
    s_ref, bias_ref = rest[2 * n_pages + 1:]
    page = k_pages[0].shape[0]
    rows = page * H_DIFF
    lanes = bias_ref.shape[-1]

    @pl.when(pl.program_id(0) == 0)
    def _init_bias():
        r = lax.broadcasted_iota(jnp.int32, (rows, lanes), 0)
        j = lax.broadcasted_iota(jnp.int32, (rows, lanes), 1)
        bias_ref[...] = jnp.where((j < 2 * H_DIFF) & (r % H_DIFF == j % H_DIFF), 0.0, NEG_INF)

    q4 = q_ref[...]
    in_map0 = lax.broadcasted_iota(jnp.int32, q4.shape, 1) < DK_DIFF
    q_cols = jnp.concatenate(
        [jnp.where(in_map0, q4, 0.0), jnp.where(in_map0, 0.0, q4),
         jnp.zeros((lanes - 2 * H_DIFF, 2 * DK_DIFF), F32)], axis=0).astype(BF16)

    kn2 = jnp.concatenate([kn_ref[...], kn_ref[...]], axis=0)
    new_rows = kn2.shape[0]
    new_bias = jnp.where(lax.broadcasted_iota(jnp.int32, (new_rows, lanes), 0) < H_DIFF,
                         bias_ref[0:new_rows, :], NEG_INF)
    s_new = _nt_dot(kn2.astype(BF16), q_cols) + new_bias
    m = jnp.max(s_new, axis=0, keepdims=True)
    for p in range(n_pages):
        k_mat = k_pages[p][...].reshape(rows, 2 * DK_DIFF).astype(BF16)
        s = _nt_dot(k_mat, q_cols) + bias_ref[...]
        s_ref[p] = s
        m = jnp.maximum(m, jnp.max(s, axis=0, keepdims=True))
    e_new = jnp.exp(s_new - m)
    l = jnp.sum(e_new, axis=0, keepdims=True)
    for p in range(n_pages):
        e = jnp.exp(s_ref[p] - m)
        s_ref[p] = e
        l = l + jnp.sum(e, axis=0, keepdims=True)
    lam = _diff_lambda_in_kernel(lq1_ref, lk1_ref, lq2_ref, lk2_ref, lam_init)
    lane = lax.broadcasted_iota(jnp.int32, (1, lanes), 1)
    coef = jnp.where(lane < H_DIFF, 1.0, -lam) / l
    coef = jnp.where(lane < 2 * H_DIFF, coef, 0.0)
    spread = jnp.ones((lanes, DV_DIFF), BF16)
    a_new = jnp.dot((e_new * coef).astype(BF16), spread, preferred_element_type=F32)
    acc = a_new * jnp.concatenate([vn_ref[...], vn_ref[...]], axis=0)
    for p in range(n_pages):
        a = jnp.dot((s_ref[p] * coef).astype(BF16), spread, preferred_element_type=F32)
        av = a * v_pages[p][...].reshape(rows, DV_DIFF)
        acc = acc + jnp.sum(av.reshape(rows // new_rows, new_rows, DV_DIFF), axis=0)
    o = acc[0:H_DIFF, :] + acc[H_DIFF:new_rows, :]
    o = o * lax.rsqrt(jnp.mean(o * o, axis=-1, keepdims=True) + GN_EPS)
    o = o * gain_ref[...] * (1.0 - lam_init)
    y_ref[...] = (o * _silu(z_ref[...])).astype(y_ref.dtype)


def _diff_sample(q4, kn4, vn4, z4, cache_k, cache_v, layer, page_table, lam_params, gain, lam_init):
    n_seq = q4.shape[0]
    n_pages = page_table.shape[1]
    page = cache_k.shape[2]
    lanes = 128
    small = pl.BlockSpec((1, DK_DIFF), lambda b, pt: (0, 0))
    tok = pl.BlockSpec((None, H_DIFF, DV_DIFF), lambda b, pt: (b, 0, 0))
    pg = lambda p: pl.BlockSpec((None, None, page, H_DIFF, DV_DIFF),
                                lambda b, pt, p=p: (layer, pt[b, p], 0, 0, 0))
    grid_spec = pltpu.PrefetchScalarGridSpec(
        num_scalar_prefetch=1,
        grid=(n_seq,),
        in_specs=[small, small, small, small,
                  pl.BlockSpec((1, DV_DIFF), lambda b, pt: (0, 0)),
                  tok, tok, tok, tok]
                 + [pg(p) for p in range(n_pages)] + [pg(p) for p in range(n_pages)],
        out_specs=tok,
        scratch_shapes=[pltpu.VMEM((n_pages, page * H_DIFF, lanes), F32),
                        pltpu.VMEM((page * H_DIFF, lanes), F32)],
    )
    return pl.pallas_call(
        functools.partial(_diff_sample_kernel, n_pages=n_pages, lam_init=lam_init),
        grid_spec=grid_spec,
        out_shape=jax.ShapeDtypeStruct((n_seq, H_DIFF, DV_DIFF), F32),
        compiler_params=_params("arbitrary"),
        name="diff_sample",
    )(page_table, *lam_params, gain, q4, kn4, vn4, z4,
      *([cache_k] * n_pages), *([cache_v] * n_pages))


def _merge_kernel(x_ref, yr_ref, ya_ref, gr_ref, ga_ref, wr_ref, wa_ref, wo_ref, lng_ref, lnb_ref, y_ref,
                  *, alpha):
    br = jnp.dot(yr_ref[...].astype(BF16), wr_ref[...], preferred_element_type=F32)
    ba = jnp.dot(ya_ref[...].astype(BF16), wa_ref[...], preferred_element_type=F32)
    m = _sigmoid(gr_ref[...].astype(F32)) * br + _sigmoid(ga_ref[...].astype(F32)) * ba
    h = alpha * x_ref[...] + jnp.dot(m.astype(BF16), wo_ref[...], preferred_element_type=F32)
    mu = jnp.mean(h, axis=-1, keepdims=True)
    cen = h - mu
    var = jnp.mean(cen * cen, axis=-1, keepdims=True)
    y_ref[...] = cen * lax.rsqrt(var + LN_EPS) * lng_ref[...] + lnb_ref[...]


def _merge(x2d, y_r, y_a, h, w_r, w_a, w_o, ln_gain, ln_bias, rows, alpha):
    m, d = x2d.shape
    assert m % rows == 0
    gate_blocks = (2 * SEG) // d if d <= 2 * SEG else None
    assert gate_blocks == 1, "gates are two segments wide, same as the model width"
    const = lambda shape: pl.BlockSpec(shape, lambda i: (0, 0), pipeline_mode=pl.Buffered(1))
    return pl.pallas_call(
        functools.partial(_merge_kernel, alpha=alpha),
        grid=(m // rows,),
        in_specs=[
            pl.BlockSpec((rows, d), lambda i: (i, 0)),
            pl.BlockSpec((rows, W_RET), lambda i: (i, 0)),
            pl.BlockSpec((rows, W_DIFF), lambda i: (i, 0)),
            pl.BlockSpec((rows, d), lambda i: (i, SEG_GR * SEG // d)),
            pl.BlockSpec((rows, d), lambda i: (i, SEG_GA * SEG // d)),
            const((W_RET, d)), const((W_DIFF, d)), const((d, d)),
            const((1, d)), const((1, d)),
        ],
        out_specs=pl.BlockSpec((rows, d), lambda i: (i, 0)),
        out_shape=jax.ShapeDtypeStruct((m, d), F32),
        compiler_params=_params("arbitrary"),
        name="merge",
    )(x2d, y_r, y_a, h, h, w_r, w_a, w_o, ln_gain, ln_bias)


def _rope_tables(pos):
    half = DK_RET // 2
    inv_freq = 1.0 / (ROPE_BASE ** (jnp.arange(half, dtype=F32) / half))
    ang = pos[:, None] * inv_freq[None, :]
    return jnp.cos(ang), jnp.sin(ang)


def kernel(x_prompt, x_sample, state_ret, cache_k, cache_v, page_table, w_in, w_branch_ret, w_branch_diff, w_out, lambda_q1, lambda_k1, lambda_q2, lambda_k2, subln_gain, ln_gain, ln_bias):
    batch, seq, d_model = x_prompt.shape
    n_seq, t_s, _ = x_sample.shape
    depth = w_in.shape[0]
    assert t_s == 1, "the sample kernels implement the one-new-token step"
    n_pool, page = cache_k.shape[1], cache_k.shape[2]
    past_len = page_table.shape[1] * page
    alpha = (2.0 * depth) ** 0.25

    log_gamma = jnp.log(1.0 - jnp.exp2(-5.0 - jnp.arange(H_RET, dtype=F32)))
    cos_p, sin_p = _rope_tables(jnp.arange(seq, dtype=F32))
    cos_s, sin_s = _rope_tables(jnp.full((n_seq,), past_len, F32))

    hp = x_prompt.reshape(batch * seq, d_model)
    hs = x_sample.reshape(n_seq, d_model)
    outs = [[] for _ in range(6)]
    for l in range(depth):
        lam_init = 0.8 - 0.6 * math.exp(-0.3 * l)
        w_in_bf = w_in[l].astype(BF16)
        w_r_bf = w_branch_ret[l].astype(BF16)
        w_a_bf = w_branch_diff[l].astype(BF16)
        w_o_bf = w_out[l].astype(BF16)
        lam_params = [p[l].reshape(1, DK_DIFF) for p in (lambda_q1, lambda_k1, lambda_q2, lambda_k2)]
        gain = subln_gain[l].reshape(1, DV_DIFF)
        lng = ln_gain[l].reshape(1, d_model)
        lnb = ln_bias[l].reshape(1, d_model)

        h_p, k_p, v_p = _in_proj(hp, w_in_bf, cos_p, sin_p, PROJ_ROWS, BF16, DK_DIFF ** -0.5 * LOG2E)
        y_r, s_p = _retention_prompt(h_p, log_gamma, batch, seq)
        y_a = _diff_prompt(h_p, lam_params, gain, lam_init, batch, seq)
        hp = _merge(hp, y_r, y_a, h_p, w_r_bf, w_a_bf, w_o_bf, lng, lnb, MERGE_ROWS, alpha)

        h_s, k_s, v_s = _in_proj(hs, w_in_bf, cos_s, sin_s, n_seq, F32, DK_DIFF ** -0.5)
        kt = h_s[:, SEG_KR * SEG:(SEG_KR + 1) * SEG].T
        y_rs, s_s = _retention_sample(h_s, kt, state_ret[l], log_gamma)
        tok = lambda seg: h_s[:, seg * SEG:(seg + 1) * SEG].reshape(n_seq, H_DIFF, DV_DIFF)
        y_as = _diff_sample(tok(SEG_QA), k_s, v_s, tok(SEG_ZA), cache_k, cache_v, l,
                            page_table, lam_params, gain, lam_init)
        hs = _merge(hs, y_rs, y_as.reshape(n_seq, W_DIFF), h_s, w_r_bf, w_a_bf, w_o_bf, lng, lnb, n_seq, alpha)

        outs[0].append(s_p)
        outs[1].append(k_p.reshape(batch, seq, H_DIFF, 2 * DK_DIFF))
        outs[2].append(v_p.reshape(batch, seq, H_DIFF, DV_DIFF))
        outs[3].append(s_s)
        outs[4].append(k_s.reshape(n_seq, t_s, H_DIFF, 2 * DK_DIFF))
        outs[5].append(v_s.reshape(n_seq, t_s, H_DIFF, DV_DIFF))

    return (hp.reshape(batch, seq, d_model), hs.reshape(n_seq, t_s, d_model),
            *[jnp.stack(o) for o in outs])
```

```python
import functools
import math

import jax
import jax.numpy as jnp
import numpy as np
from jax import lax
from jax.experimental import pallas as pl
from jax.experimental.pallas import tpu as pltpu

F32 = jnp.float32
BF16 = jnp.bfloat16

H_RET = 4
DK_RET = 256
DV_RET = 256
H_DIFF = 4
DK_DIFF = 128
DV_DIFF = 256
ROPE_BASE = 10000.0
LN_EPS = 1e-5
GN_EPS = 1e-6
NEG_INF = -1e30

SEG = 1024
W_RET = H_RET * DV_RET
W_DIFF = H_DIFF * DV_DIFF
SEG_QR, SEG_KR, SEG_VR, SEG_ZR, SEG_QA, SEG_KA, SEG_VA, SEG_ZA = range(8)
SEG_GR = 8
SEG_GA = 10
N_SEG = 12

LANES = 128
LOG2E = math.log2(math.e)
VMEM_LIMIT_BYTES = 56 * 1024 * 1024

RET_CHUNK = 256
Q_BLOCK = 512
PROJ_ROWS = 512
MERGE_ROWS = 256
SAMPLE_GROUP = 16


def _params(*sem):
    return pltpu.CompilerParams(dimension_semantics=sem, vmem_limit_bytes=VMEM_LIMIT_BYTES)


def _silu(z):
    return z * (1.0 / (1.0 + jnp.exp(-z)))


def _sigmoid(z):
    return 1.0 / (1.0 + jnp.exp(-z))


def _nt_dot(a, b):
    return lax.dot_general(a, b, (((1,), (1,)), ((), ())), preferred_element_type=F32)


def _store_segment(acc, j, cos_ref, sin_ref, h_ref, k_ref, v_ref, q_scale):
    @pl.when(j <= SEG_KR)
    def _rope():
        scale = jnp.where(j == SEG_KR, DK_RET ** -0.5, 1.0).astype(F32)
        cos = cos_ref[...] * scale
        sin = sin_ref[...] * scale
        half = DK_RET // 2
        for hh in range(H_RET):
            a = hh * DK_RET
            x1 = acc[:, a:a + half]
            x2 = acc[:, a + half:a + DK_RET]
            h_ref[:, a:a + half] = (x1 * cos - x2 * sin).astype(h_ref.dtype)
            h_ref[:, a + half:a + DK_RET] = (x2 * cos + x1 * sin).astype(h_ref.dtype)

    @pl.when(j == SEG_QA)
    def _scaled():
        h_ref[...] = (acc * q_scale).astype(h_ref.dtype)

    @pl.when(j == SEG_KA)
    def _keys():
        h_ref[...] = acc.astype(h_ref.dtype)
        k_ref[...] = acc.reshape(k_ref.shape)

    @pl.when(j == SEG_VA)
    def _values():
        h_ref[...] = acc.astype(h_ref.dtype)
        v_ref[...] = acc.reshape(v_ref.shape)

    @pl.when((j == SEG_VR) | (j == SEG_ZR) | (j >= SEG_ZA))
    def _plain():
        h_ref[...] = acc.astype(h_ref.dtype)


def _in_proj_kernel(x_ref, xs_ref, w_ref, cos_ref, sin_ref, cos_s_ref, sin_s_ref,
                    h_ref, k_ref, v_ref, hs_ref, ks_ref, vs_ref, wb_ref, *, q_scale, q_scale_s):
    j = pl.program_id(0)
    i = pl.program_id(1)

    @pl.when(i == 0)
    def _cast_weights():
        wb_ref[...] = w_ref[...].astype(BF16)

    acc = jnp.dot(x_ref[...].astype(BF16), wb_ref[...], preferred_element_type=F32)
    _store_segment(acc, j, cos_ref, sin_ref, h_ref, k_ref, v_ref, q_scale)

    @pl.when(i == 0)
    def _sample_rows():
        acc_s = jnp.dot(xs_ref[...].astype(BF16), wb_ref[...], preferred_element_type=F32)
        _store_segment(acc_s, j, cos_s_ref, sin_s_ref, hs_ref, ks_ref, vs_ref, q_scale_s)


def _in_proj(x2d, xs2d, w, cos, sin, cos_s, sin_s, rows, q_scale, q_scale_s):
    m, d = x2d.shape
    n_s = xs2d.shape[0]
    n = w.shape[1]
    assert n == N_SEG * SEG and m % rows == 0 and cos.shape[0] % rows == 0
    n_i = m // rows
    pos_blocks = cos.shape[0] // rows
    half = DK_RET // 2

    def kv_block(seg):
        return lambda j, i: (jnp.where(j == seg, i, jnp.where(j < seg, 0, n_i - 1)), 0, 0)

    const2 = lambda j, i: (0, 0)
    const3 = lambda j, i: (0, 0, 0)
    return pl.pallas_call(
        functools.partial(_in_proj_kernel, q_scale=q_scale, q_scale_s=q_scale_s),
        grid=(N_SEG, n_i),
        in_specs=[
            pl.BlockSpec((rows, d), lambda j, i: (i, 0)),
            pl.BlockSpec((n_s, d), const2),
            pl.BlockSpec((d, SEG), lambda j, i: (0, j)),
            pl.BlockSpec((rows, half), lambda j, i: (i % pos_blocks, 0)),
            pl.BlockSpec((rows, half), lambda j, i: (i % pos_blocks, 0)),
            pl.BlockSpec((n_s, half), const2),
            pl.BlockSpec((n_s, half), const2),
        ],
        out_specs=[
            pl.BlockSpec((rows, SEG), lambda j, i: (i, j)),
            pl.BlockSpec((rows, H_DIFF, 2 * DK_DIFF), kv_block(SEG_KA)),
            pl.BlockSpec((rows, H_DIFF, DV_DIFF), kv_block(SEG_VA)),
            pl.BlockSpec((n_s, SEG), lambda j, i: (0, j)),
            pl.BlockSpec((n_s, H_DIFF, 2 * DK_DIFF), const3),
            pl.BlockSpec((n_s, H_DIFF, DV_DIFF), const3),
        ],
        out_shape=[
            jax.ShapeDtypeStruct((m, n), BF16),
            jax.ShapeDtypeStruct((m, H_DIFF, 2 * DK_DIFF), F32),
            jax.ShapeDtypeStruct((m, H_DIFF, DV_DIFF), F32),
            jax.ShapeDtypeStruct((n_s, n), F32),
            jax.ShapeDtypeStruct((n_s, H_DIFF, 2 * DK_DIFF), F32),
            jax.ShapeDtypeStruct((n_s, H_DIFF, DV_DIFF), F32),
        ],
        scratch_shapes=[pltpu.VMEM((d, SEG), BF16)],
        compiler_params=_params("arbitrary", "arbitrary"),
        name="in_proj",
    )(x2d, xs2d, w, cos, sin, cos_s, sin_s)


def _retention_prompt_kernel(lg_ref, q_ref, k_ref, v_ref, z_ref, y_ref, s_out_ref,
                             s_ref, decay_ref, cross_ref, kdec_ref):
    c = pl.program_id(1)
    chunk = q_ref.shape[0]

    @pl.when(c == 0)
    def _init():
        s_ref[...] = jnp.zeros_like(s_ref)
        row = lax.broadcasted_iota(jnp.int32, (chunk, chunk), 0)
        col = lax.broadcasted_iota(jnp.int32, (chunk, chunk), 1)
        diff = (row - col).astype(F32)
        pos = lax.broadcasted_iota(jnp.int32, (chunk, DV_RET), 0).astype(F32)
        for hh in range(H_RET):
            lg = lg_ref[hh]
            decay_ref[hh] = jnp.where(diff >= 0, jnp.exp(lg * jnp.maximum(diff, 0.0)), 0.0)
            cross_ref[hh] = jnp.exp(lg * (pos + 1.0))
            kdec_ref[hh] = jnp.exp(lg * (chunk - 1.0 - pos))

    for hh in range(H_RET):
        sl = slice(hh * DK_RET, (hh + 1) * DK_RET)
        q = q_ref[:, sl]
        k = k_ref[:, sl]
        v = v_ref[:, sl]
        s_old = s_ref[hh]
        intra = (_nt_dot(q, k) * decay_ref[hh]).astype(BF16)
        o = jnp.dot(intra, v, preferred_element_type=F32)
        o = o + jnp.dot(q, s_old.astype(BF16), preferred_element_type=F32) * cross_ref[hh]
        k_dec = (k.astype(F32) * kdec_ref[hh]).astype(BF16)
        state_decay = jnp.exp(jnp.full((1, DV_RET), lg_ref[hh] * chunk, F32))
        s_new = s_old * state_decay + lax.dot_general(
            k_dec, v, (((0,), (0,)), ((), ())), preferred_element_type=F32)
        s_ref[hh] = s_new
        mu = jnp.mean(o, axis=-1, keepdims=True)
        cen = o - mu
        var = jnp.mean(cen * cen, axis=-1, keepdims=True)
        gate = _silu(z_ref[:, sl].astype(F32))
        y_ref[:, sl] = (cen * lax.rsqrt(var + GN_EPS) * gate).astype(y_ref.dtype)

    @pl.when(c == pl.num_programs(1) - 1)
    def _emit_state():
        s_out_ref[...] = s_ref[...]


def _retention_prompt(h, log_gamma, batch, seq):
    chunk = RET_CHUNK
    assert seq % chunk == 0
    nc = seq // chunk
    blk = lambda seg: pl.BlockSpec((chunk, SEG), lambda b, c, seg=seg: (b * nc + c, seg))
    return pl.pallas_call(
        _retention_prompt_kernel,
        grid=(batch, nc),
        in_specs=[pl.BlockSpec(memory_space=pltpu.SMEM),
                  blk(SEG_QR), blk(SEG_KR), blk(SEG_VR), blk(SEG_ZR)],
        out_specs=[
            pl.BlockSpec((chunk, W_RET), lambda b, c: (b * nc + c, 0)),
            pl.BlockSpec((None, H_RET, DK_RET, DV_RET), lambda b, c: (b, 0, 0, 0)),
        ],
        out_shape=[
            jax.ShapeDtypeStruct((batch * seq, W_RET), BF16),
            jax.ShapeDtypeStruct((batch, H_RET, DK_RET, DV_RET), F32),
        ],
        scratch_shapes=[
            pltpu.VMEM((H_RET, DK_RET, DV_RET), F32),
            pltpu.VMEM((H_RET, chunk, chunk), F32),
            pltpu.VMEM((H_RET, chunk, DV_RET), F32),
            pltpu.VMEM((H_RET, chunk, DK_RET), F32),
        ],
        compiler_params=_params("arbitrary", "arbitrary"),
        name="retention_prompt",
    )(log_gamma, h, h, h, h)


def _diff_lambda_in_kernel(lq1_ref, lk1_ref, lq2_ref, lk2_ref, lam_init):
    a = jnp.sum(lq1_ref[...] * lk1_ref[...], axis=-1, keepdims=True)
    b = jnp.sum(lq2_ref[...] * lk2_ref[...], axis=-1, keepdims=True)
    return jnp.exp(a) - jnp.exp(b) + lam_init


def _diff_prompt_kernel(lq1_ref, lk1_ref, lq2_ref, lk2_ref, gain_ref, q_ref, k_ref, v_ref, z_ref, y_ref,
                        sa_ref, sb_ref, m_ref, l_ref, acc_ref, *, lam_init):
    i = pl.program_id(2)
    blk = q_ref.shape[0]
    lanes = m_ref.shape[-1]
    qs = (q_ref[:, :DK_DIFF], q_ref[:, DK_DIFF:])

    m_ref[...] = jnp.full_like(m_ref, NEG_INF)
    l_ref[...] = jnp.zeros_like(l_ref)
    acc_ref[...] = jnp.zeros_like(acc_ref)

    def scores_into(buf_ref, j):
        start = pl.multiple_of(j * blk, blk)
        kb = k_ref[pl.ds(start, blk), :]
        for c in range(2):
            buf_ref[c] = _nt_dot(qs[c], kb[:, c * DK_DIFF:(c + 1) * DK_DIFF])

    def softmax_pv(buf_ref, j, masked):
        start = pl.multiple_of(j * blk, blk)
        vb = v_ref[pl.ds(start, blk), :]
        if masked:
            row = lax.broadcasted_iota(jnp.int32, (blk, blk), 0)
            col = lax.broadcasted_iota(jnp.int32, (blk, blk), 1)
            keep = col <= row
        for c in range(2):
            s = buf_ref[c]
            if masked:
                s = jnp.where(keep, s, NEG_INF)
            m_old = m_ref[c]
            m_new = jnp.maximum(m_old, jnp.max(s, axis=-1, keepdims=True))
            alpha = jnp.exp2(m_old - m_new)
            p = jnp.exp2(s - pltpu.repeat(m_new, blk // lanes, axis=1))
            part = p[:, 0:lanes]
            for t in range(1, blk // lanes):
                part = part + p[:, t * lanes:(t + 1) * lanes]
            l_ref[c] = alpha * l_ref[c] + part
            acc_ref[c] = (pltpu.repeat(alpha, DV_DIFF // lanes, axis=1) * acc_ref[c]
                          + jnp.dot(p.astype(BF16), vb, preferred_element_type=F32))
            m_ref[c] = m_new

    scores_into(sa_ref, 0)

    def pair(t, carry):
        j = 2 * t
        scores_into(sb_ref, j + 1)
        softmax_pv(sa_ref, j, False)
        scores_into(sa_ref, j + 2)
        softmax_pv(sb_ref, j + 1, False)
        return carry

    lax.fori_loop(0, i // 2, pair, 0)

    @pl.when(i % 2 == 1)
    def _odd_tail():
        scores_into(sb_ref, i)
        softmax_pv(sa_ref, i - 1, False)
        softmax_pv(sb_ref, i, True)

    @pl.when(i % 2 == 0)
    def _even_tail():
        softmax_pv(sa_ref, i, True)

    lam = _diff_lambda_in_kernel(lq1_ref, lk1_ref, lq2_ref, lk2_ref, lam_init)
    l0 = jnp.sum(l_ref[0], axis=-1, keepdims=True)
    l1 = jnp.sum(l_ref[1], axis=-1, keepdims=True)
    o = acc_ref[0] / l0 - lam * (acc_ref[1] / l1)
    o = o * lax.rsqrt(jnp.mean(o * o, axis=-1, keepdims=True) + GN_EPS)
    o = o * gain_ref[...] * (1.0 - lam_init)
    y_ref[...] = (o * _silu(z_ref[...].astype(F32))).astype(y_ref.dtype)


def _diff_prompt(h, lam_params, gain, lam_init, batch, seq):
    blk = Q_BLOCK
    assert seq % blk == 0
    nq = seq // blk
    heads_per_seg = SEG // DV_DIFF
    small = pl.BlockSpec((1, DK_DIFF), lambda b, hh, i: (0, 0))
    return pl.pallas_call(
        functools.partial(_diff_prompt_kernel, lam_init=lam_init),
        grid=(batch, H_DIFF, nq),
        in_specs=[
            small, small, small, small,
            pl.BlockSpec((1, DV_DIFF), lambda b, hh, i: (0, 0)),
            pl.BlockSpec((blk, 2 * DK_DIFF), lambda b, hh, i: (b * nq + i, SEG_QA * heads_per_seg + hh)),
            pl.BlockSpec((seq, 2 * DK_DIFF), lambda b, hh, i: (b, SEG_KA * heads_per_seg + hh)),
            pl.BlockSpec((seq, DV_DIFF), lambda b, hh, i: (b, SEG_VA * heads_per_seg + hh)),
            pl.BlockSpec((blk, DV_DIFF), lambda b, hh, i: (b * nq + i, SEG_ZA * heads_per_seg + hh)),
        ],
        out_specs=pl.BlockSpec((blk, DV_DIFF), lambda b, hh, i: (b * nq + i, hh)),
        out_shape=jax.ShapeDtypeStruct((batch * seq, W_DIFF), BF16),
        scratch_shapes=[
            pltpu.VMEM((2, blk, blk), F32),
            pltpu.VMEM((2, blk, blk), F32),
            pltpu.VMEM((2, blk, LANES), F32),
            pltpu.VMEM((2, blk, LANES), F32),
            pltpu.VMEM((2, blk, DV_DIFF), F32),
        ],
        compiler_params=_params("arbitrary", "arbitrary", "arbitrary"),
        name="diff_prompt",
    )(*lam_params, gain, h, h, h, h)


def _retention_sample_kernel(lg_ref, q_ref, k_ref, kt_ref, v_ref, vall_ref, z_ref, s_in_ref,
                             y_ref, s_out_ref):
    g = pl.program_id(0)
    hh = pl.program_id(1)
    group = q_ref.shape[0]
    n_seq = kt_ref.shape[1]
    gamma = jnp.exp(jnp.full((1, DV_RET), lg_ref[hh], F32))
    q_bf = q_ref[...].astype(BF16)
    k_bf = k_ref[...].astype(BF16)
    v_bf = v_ref[...].astype(BF16)
    qk = jnp.sum(q_bf.astype(F32) * k_bf.astype(F32), axis=-1, keepdims=True)
    kt_bf = kt_ref[...].astype(BF16)
    vall_bf = vall_ref[...].astype(BF16)
    seq_lane = lax.broadcasted_iota(jnp.int32, (DK_RET, n_seq), 1)
    row = lax.broadcasted_iota(jnp.int32, (group, DV_RET), 0)
    cross = jnp.zeros((group, DV_RET), F32)
    for s in range(group):
        s_old = s_in_ref[s]
        qs = jnp.dot(q_bf, s_old.astype(BF16), preferred_element_type=F32)
        cross = jnp.where(row == s, qs, cross)
        k_only = jnp.where(seq_lane == g * group + s, kt_bf, jnp.zeros_like(kt_bf))
        outer = jnp.dot(k_only, vall_bf, preferred_element_type=F32)
        s_out_ref[s] = s_old * gamma + outer
    o = qk * v_bf.astype(F32) + cross * gamma
    mu = jnp.mean(o, axis=-1, keepdims=True)
    cen = o - mu
    var = jnp.mean(cen * cen, axis=-1, keepdims=True)
    y_ref[...] = (cen * lax.rsqrt(var + GN_EPS) * _silu(z_ref[...].astype(F32))).astype(y_ref.dtype)


def _retention_sample(hs, kt, state, log_gamma):
    n_seq = hs.shape[0]
    group = SAMPLE_GROUP
    assert n_seq % group == 0
    per_seg = SEG // DK_RET
    col = lambda seg: pl.BlockSpec((group, DK_RET), lambda g, hh, seg=seg: (g, seg * per_seg + hh))
    st = pl.BlockSpec((group, None, DK_RET, DV_RET), lambda g, hh: (g, hh, 0, 0))
    return pl.pallas_call(
        _retention_sample_kernel,
        grid=(n_seq // group, H_RET),
        in_specs=[
            pl.BlockSpec(memory_space=pltpu.SMEM),
            col(SEG_QR), col(SEG_KR),
            pl.BlockSpec((DK_RET, n_seq), lambda g, hh: (hh, 0)),
            col(SEG_VR),
            pl.BlockSpec((n_seq, DV_RET), lambda g, hh: (0, SEG_VR * per_seg + hh)),
            col(SEG_ZR),
            st,
        ],
        out_specs=[pl.BlockSpec((group, DV_RET), lambda g, hh: (g, hh)), st],
        out_shape=[
            jax.ShapeDtypeStruct((n_seq, W_RET), F32),
            jax.ShapeDtypeStruct(state.shape, F32),
        ],
        compiler_params=_params("arbitrary", "arbitrary"),
        name="retention_sample",
    )(log_gamma, hs, hs, kt, hs, hs, hs, state)


def _diff_sample_kernel(pt_ref, lq1_ref, lk1_ref, lq2_ref, lk2_ref, gain_ref, q_ref, kn_ref, vn_ref, z_ref,
                        *rest, n_pages, lam_init):
    k_pages = rest[:n_pages]
    v_pages = rest[n_pages:2 * n_pages]
    y_ref = rest[2 * n_pages]
    s_ref, bias_ref = rest[2 * n_pages + 1:]
    page = k_pages[0].shape[0]
    rows = page * H_DIFF
    lanes = bias_ref.shape[-1]

    @pl.when(pl.program_id(0) == 0)
    def _init_bias():
        r = lax.broadcasted_iota(jnp.int32, (rows, lanes), 0)
        j = lax.broadcasted_iota(jnp.int32, (rows, lanes), 1)
        bias_ref[...] = jnp.where((j < 2 * H_DIFF) & (r % H_DIFF == j % H_DIFF), 0.0, NEG_INF)

    q4 = q_ref[...]
    in_map0 = lax.broadcasted_iota(jnp.int32, q4.shape, 1) < DK_DIFF
    q_cols = jnp.concatenate(
        [jnp.where(in_map0, q4, 0.0), jnp.where(in_map0, 0.0, q4),
         jnp.zeros((lanes - 2 * H_DIFF, 2 * DK_DIFF), F32)], axis=0).astype(BF16)

    kn2 = jnp.concatenate([kn_ref[...], kn_ref[...]], axis=0)
    new_rows = kn2.shape[0]
    new_bias = jnp.where(lax.broadcasted_iota(jnp.int32, (new_rows, lanes), 0) < H_DIFF,
                         bias_ref[0:new_rows, :], NEG_INF)
    s_new = _nt_dot(kn2.astype(BF16), q_cols) + new_bias
    m = jnp.max(s_new, axis=0, keepdims=True)
    for p in range(n_pages):
        k_mat = k_pages[p][...].reshape(rows, 2 * DK_DIFF).astype(BF16)
        s = _nt_dot(k_mat, q_cols) + bias_ref[...]
        s_ref[p] = s
        m = jnp.maximum(m, jnp.max(s, axis=0, keepdims=True))
    e_new = jnp.exp(s_new - m)
    l = jnp.sum(e_new, axis=0, keepdims=True)
    for p in range(n_pages):
        e = jnp.exp(s_ref[p] - m)
        s_ref[p] = e
        l = l + jnp.sum(e, axis=0, keepdims=True)
    lam = _diff_lambda_in_kernel(lq1_ref, lk1_ref, lq2_ref, lk2_ref, lam_init)
    lane = lax.broadcasted_iota(jnp.int32, (1, lanes), 1)
    coef = jnp.where(lane < H_DIFF, 1.0, -lam) / l
    coef = jnp.where(lane < 2 * H_DIFF, coef, 0.0)
    spread = jnp.ones((lanes, DV_DIFF), BF16)
    a_new = jnp.dot((e_new * coef).astype(BF16), spread, preferred_element_type=F32)
    acc = a_new * jnp.concatenate([vn_ref[...], vn_ref[...]], axis=0)
    for p in range(n_pages):
        a = jnp.dot((s_ref[p] * coef).astype(BF16), spread, preferred_element_type=F32)
        av = a * v_pages[p][...].reshape(rows, DV_DIFF)
        acc = acc + jnp.sum(av.reshape(rows // new_rows, new_rows, DV_DIFF), axis=0)
    o = acc[0:H_DIFF, :] + acc[H_DIFF:new_rows, :]
    o = o * lax.rsqrt(jnp.mean(o * o, axis=-1, keepdims=True) + GN_EPS)
    o = o * gain_ref[...] * (1.0 - lam_init)
    y_ref[...] = (o * _silu(z_ref[...])).astype(y_ref.dtype)


def _diff_sample(q4, kn4, vn4, z4, cache_k, cache_v, layer, page_table, lam_params, gain, lam_init):
    n_seq = q4.shape[0]
    n_pages = page_table.shape[1]
    page = cache_k.shape[2]
    lanes = 128
    small = pl.BlockSpec((1, DK_DIFF), lambda b, pt: (0, 0))
    tok = pl.BlockSpec((None, H_DIFF, DV_DIFF), lambda b, pt: (b, 0, 0))
    pg = lambda p: pl.BlockSpec((None, None, page, H_DIFF, DV_DIFF),
                                lambda b, pt, p=p: (layer, pt[b, p], 0, 0, 0))
    grid_spec = pltpu.PrefetchScalarGridSpec(
        num_scalar_prefetch=1,
        grid=(n_seq,),
        in_specs=[small, small, small, small,
                  pl.BlockSpec((1, DV_DIFF), lambda b, pt: (0, 0)),
                  tok, tok, tok, tok]
                 + [pg(p) for p in range(n_pages)] + [pg(p) for p in range(n_pages)],
        out_specs=tok,
        scratch_shapes=[pltpu.VMEM((n_pages, page * H_DIFF, lanes), F32),
                        pltpu.VMEM((page * H_DIFF, lanes), F32)],
    )
    return pl.pallas_call(
        functools.partial(_diff_sample_kernel, n_pages=n_pages, lam_init=lam_init),
        grid_spec=grid_spec,
        out_shape=jax.ShapeDtypeStruct((n_seq, H_DIFF, DV_DIFF), F32),
        compiler_params=_params("arbitrary"),
        name="diff_sample",
    )(page_table, *lam_params, gain, q4, kn4, vn4, z4,
      *([cache_k] * n_pages), *([cache_v] * n_pages))


def _merge_kernel(x_ref, yr_ref, ya_ref, gr_ref, ga_ref, wr_ref, wa_ref, wo_ref, lng_ref, lnb_ref, y_ref,
                  *, alpha):
    br = jnp.dot(yr_ref[...].astype(BF16), wr_ref[...], preferred_element_type=F32)
    ba = jnp.dot(ya_ref[...].astype(BF16), wa_ref[...], preferred_element_type=F32)
    m = _sigmoid(gr_ref[...].astype(F32)) * br + _sigmoid(ga_ref[...].astype(F32)) * ba
    h = alpha * x_ref[...] + jnp.dot(m.astype(BF16), wo_ref[...], preferred_element_type=F32)
    mu = jnp.mean(h, axis=-1, keepdims=True)
    cen = h - mu
    var = jnp.mean(cen * cen, axis=-1, keepdims=True)
    y_ref[...] = cen * lax.rsqrt(var + LN_EPS) * lng_ref[...] + lnb_ref[...]


def _merge(x2d, y_r, y_a, h, w_r, w_a, w_o, ln_gain, ln_bias, rows, alpha):
    m, d = x2d.shape
    assert m % rows == 0
    gate_blocks = (2 * SEG) // d if d <= 2 * SEG else None
    assert gate_blocks == 1, "gates are two segments wide, same as the model width"
    const = lambda shape: pl.BlockSpec(shape, lambda i: (0, 0), pipeline_mode=pl.Buffered(1))
    return pl.pallas_call(
        functools.partial(_merge_kernel, alpha=alpha),
        grid=(m // rows,),
        in_specs=[
            pl.BlockSpec((rows, d), lambda i: (i, 0)),
            pl.BlockSpec((rows, W_RET), lambda i: (i, 0)),
            pl.BlockSpec((rows, W_DIFF), lambda i: (i, 0)),
            pl.BlockSpec((rows, d), lambda i: (i, SEG_GR * SEG // d)),
            pl.BlockSpec((rows, d), lambda i: (i, SEG_GA * SEG // d)),
            const((W_RET, d)), const((W_DIFF, d)), const((d, d)),
            const((1, d)), const((1, d)),
        ],
        out_specs=pl.BlockSpec((rows, d), lambda i: (i, 0)),
        out_shape=jax.ShapeDtypeStruct((m, d), F32),
        compiler_params=_params("arbitrary"),
        name="merge",
    )(x2d, y_r, y_a, h, h, w_r, w_a, w_o, ln_gain, ln_bias)


def _rope_tables(pos):
    half = DK_RET // 2
    inv_freq = 1.0 / (ROPE_BASE ** (jnp.arange(half, dtype=F32) / half))
    ang = pos[:, None] * inv_freq[None, :]
    return jnp.cos(ang), jnp.sin(ang)


def kernel(x_prompt, x_sample, state_ret, cache_k, cache_v, page_table, w_in, w_branch_ret, w_branch_diff, w_out, lambda_q1, lambda_k1, lambda_q2, lambda_k2, subln_gain, ln_gain, ln_bias):
    batch, seq, d_model = x_prompt.shape
    n_seq, t_s, _ = x_sample.shape
    depth = w_in.shape[0]
    assert t_s == 1, "the sample kernels implement the one-new-token step"
    n_pool, page = cache_k.shape[1], cache_k.shape[2]
    past_len = page_table.shape[1] * page
    alpha = (2.0 * depth) ** 0.25

    log_gamma = jnp.log(1.0 - jnp.exp2(-5.0 - jnp.arange(H_RET, dtype=F32)))
    cos_p, sin_p = _rope_tables(jnp.arange(seq, dtype=F32))
    cos_s, sin_s = _rope_tables(jnp.full((n_seq,), past_len, F32))

    hp = x_prompt.reshape(batch * seq, d_model)
    hs = x_sample.reshape(n_seq, d_model)
    outs = [[] for _ in range(6)]
    for l in range(depth):
        lam_init = 0.8 - 0.6 * math.exp(-0.3 * l)
        w_r_bf = w_branch_ret[l].astype(BF16)
        w_a_bf = w_branch_diff[l].astype(BF16)
        w_o_bf = w_out[l].astype(BF16)
        lam_params = [p[l].reshape(1, DK_DIFF) for p in (lambda_q1, lambda_k1, lambda_q2, lambda_k2)]
        gain = subln_gain[l].reshape(1, DV_DIFF)
        lng = ln_gain[l].reshape(1, d_model)
        lnb = ln_bias[l].reshape(1, d_model)

        h_p, k_p, v_p, h_s, k_s, v_s = _in_proj(hp, hs, w_in[l], cos_p, sin_p, cos_s, sin_s, PROJ_ROWS,
                                                DK_DIFF ** -0.5 * LOG2E, DK_DIFF ** -0.5)

        y_r, s_p = _retention_prompt(h_p, log_gamma, batch, seq)
        y_a = _diff_prompt(h_p, lam_params, gain, lam_init, batch, seq)
        hp = _merge(hp, y_r, y_a, h_p, w_r_bf, w_a_bf, w_o_bf, lng, lnb, MERGE_ROWS, alpha)

        kt = h_s[:, SEG_KR * SEG:(SEG_KR + 1) * SEG].T
        y_rs, s_s = _retention_sample(h_s, kt, state_ret[l], log_gamma)
        tok = lambda seg: h_s[:, seg * SEG:(seg + 1) * SEG].reshape(n_seq, H_DIFF, DV_DIFF)
        y_as = _diff_sample(tok(SEG_QA), k_s, v_s, tok(SEG_ZA), cache_k, cache_v, l,
                            page_table, lam_params, gain, lam_init)
        hs = _merge(hs, y_rs, y_as.reshape(n_seq, W_DIFF), h_s, w_r_bf, w_a_bf, w_o_bf, lng, lnb, n_seq, alpha)

        outs[0].append(s_p)
        outs[1].append(k_p.reshape(batch, seq, H_DIFF, 2 * DK_DIFF))
        outs[2].append(v_p.reshape(batch, seq, H_DIFF, DV_DIFF))
        outs[3].append(s_s)
        outs[4].append(k_s.reshape(n_seq, t_s, H_DIFF, 2 * DK_DIFF))
        outs[5].append(v_s.reshape(n_seq, t_s, H_DIFF, DV_DIFF))

    return (hp.reshape(batch, seq, d_model), hs.reshape(n_seq, t_s, d_model),
            *[jnp.stack(o) for o in outs])
```

```python
import functools
import math

import jax
import jax.numpy as jnp
import numpy as np
from jax import lax
from jax.experimental import pallas as pl
from jax.experimental.pallas import tpu as pltpu

F32 = jnp.float32
BF16 = jnp.bfloat16

H_RET = 4
DK_RET = 256
DV_RET = 256
H_DIFF = 4
DK_DIFF = 128
DV_DIFF = 256
ROPE_BASE = 10000.0
LN_EPS = 1e-5
GN_EPS = 1e-6
NEG_INF = -1e30

SEG = 1024
W_RET = H_RET * DV_RET
W_DIFF = H_DIFF * DV_DIFF
W_SEG_KA = 5
SEG_QR, SEG_KR, SEG_VR, SEG_ZR, SEG_QA, SEG_ZA = range(6)
SEG_GR = 6
SEG_GA = 8
N_SEG = 10

LANES = 128
LOG2E = math.log2(math.e)
VMEM_LIMIT_BYTES = 56 * 1024 * 1024

RET_CHUNK = 256
Q_BLOCK = 512
KV_PROJ_ROWS = 256
PROJ_ROWS = 1024
MERGE_ROWS = 512
MERGE_PARTS = 2
MERGE_MIN_PART = 256
SAMPLE_GROUP = 32
PAGE_SLOTS = 3
DECODE_VMEM_LIMIT_BYTES = 60 * 1024 * 1024


def _params(*sem):
    return pltpu.CompilerParams(dimension_semantics=sem, vmem_limit_bytes=VMEM_LIMIT_BYTES)


def _silu(z):
    return z * (1.0 / (1.0 + jnp.exp(-z)))


def _sigmoid(z):
    return 1.0 / (1.0 + jnp.exp(-z))


def _nt_dot(a, b):
    return lax.dot_general(a, b, (((1,), (1,)), ((), ())), preferred_element_type=F32)


def _kv_proj_kernel(x_ref, xs_ref, wk_ref, wv_ref, hkv_ref, k_ref, v_ref, xb_ref, ks_ref, vs_ref, wkb_ref, wvb_ref,
                    *, side_refs, side_steps):
    i = pl.program_id(0)

    @pl.when(i == 0)
    def _cast_weights():
        wkb_ref[...] = wk_ref[...].astype(BF16)
        wvb_ref[...] = wv_ref[...].astype(BF16)

    xb = x_ref[...].astype(BF16)
    xb_ref[...] = xb
    acc_k = jnp.dot(xb, wkb_ref[...], preferred_element_type=F32)
    hkv_ref[:, :SEG] = acc_k.astype(hkv_ref.dtype)
    k_ref[...] = acc_k.reshape(k_ref.shape)
    acc_v = jnp.dot(xb, wvb_ref[...], preferred_element_type=F32)
    hkv_ref[:, SEG:] = acc_v.astype(hkv_ref.dtype)
    v_ref[...] = acc_v.reshape(v_ref.shape)

    @pl.when(i == 0)
    def _sample_rows():
        xs = xs_ref[...].astype(BF16)
        ks_ref[...] = jnp.dot(xs, wkb_ref[...], preferred_element_type=F32).reshape(ks_ref.shape)
        vs_ref[...] = jnp.dot(xs, wvb_ref[...], preferred_element_type=F32).reshape(vs_ref.shape)

    n_side = len(side_refs) // 2
    for t in range(n_side):
        first, count = side_steps[t]

        @pl.when((i >= first) & (i < first + count))
        def _cast_side(t=t):
            side_refs[n_side + t][...] = side_refs[t][...].astype(BF16)


def _kv_proj(x2d, xs2d, w, side_weights, rows):
    m, d = x2d.shape
    n_s = xs2d.shape[0]
    assert m % rows == 0
    n_i = m // rows
    side_rows = [sw.shape[0] for sw in side_weights]
    chunk = sum(side_rows) // n_i
    assert chunk * n_i == sum(side_rows) and chunk % 16 == 0 and all(r % chunk == 0 for r in side_rows)
    side_steps, first = [], 0
    for r in side_rows:
        side_steps.append((first, r // chunk))
        first += r // chunk

    def side_spec(t):
        first, count = side_steps[t]
        return pl.BlockSpec((chunk, side_weights[t].shape[1]),
                            lambda i: (jnp.clip(i - first, 0, count - 1), 0))

    side_specs = [side_spec(t) for t in range(len(side_weights))]
    resident = lambda shape, idx: pl.BlockSpec(shape, lambda i: idx, pipeline_mode=pl.Buffered(1))

    def body(x_ref, xs_ref, wk_ref, wv_ref, *rest):
        n_side = len(side_weights)
        side_in = rest[:n_side]
        hkv_ref, k_ref, v_ref, xb_ref, ks_ref, vs_ref = rest[n_side:n_side + 6]
        side_out = rest[n_side + 6:2 * n_side + 6]
        wkb_ref, wvb_ref = rest[2 * n_side + 6:]
        _kv_proj_kernel(x_ref, xs_ref, wk_ref, wv_ref, hkv_ref, k_ref, v_ref, xb_ref, ks_ref, vs_ref,
                        wkb_ref, wvb_ref, side_refs=tuple(side_in) + tuple(side_out), side_steps=tuple(side_steps))

    return pl.pallas_call(
        body,
        grid=(n_i,),
        in_specs=[
            pl.BlockSpec((rows, d), lambda i: (i, 0)),
            resident((n_s, d), (0, 0)),
            resident((d, SEG), (0, W_SEG_KA)),
            resident((d, SEG), (0, W_SEG_KA + 1)),
        ] + side_specs,
        out_specs=[
            pl.BlockSpec((rows, 2 * SEG), lambda i: (i, 0)),
            pl.BlockSpec((rows, H_DIFF, 2 * DK_DIFF), lambda i: (i, 0, 0)),
            pl.BlockSpec((rows, H_DIFF, DV_DIFF), lambda i: (i, 0, 0)),
            pl.BlockSpec((rows, d), lambda i: (i, 0)),
            pl.BlockSpec((n_s, H_DIFF, 2 * DK_DIFF), lambda i: (0, 0, 0)),
            pl.BlockSpec((n_s, H_DIFF, DV_DIFF), lambda i: (0, 0, 0)),
        ] + side_specs,
        out_shape=[
            jax.ShapeDtypeStruct((m, 2 * SEG), BF16),
            jax.ShapeDtypeStruct((m, H_DIFF, 2 * DK_DIFF), F32),
            jax.ShapeDtypeStruct((m, H_DIFF, DV_DIFF), F32),
            jax.ShapeDtypeStruct((m, d), BF16),
            jax.ShapeDtypeStruct((n_s, H_DIFF, 2 * DK_DIFF), F32),
            jax.ShapeDtypeStruct((n_s, H_DIFF, DV_DIFF), F32),
        ] + [jax.ShapeDtypeStruct(sw.shape, BF16) for sw in side_weights],
        scratch_shapes=[pltpu.VMEM((d, SEG), BF16), pltpu.VMEM((d, SEG), BF16)],
        compiler_params=_params("arbitrary"),
        name="kv_proj",
    )(x2d, xs2d, w, w, *side_weights)


def _store_segment(acc, j, cos_ref, sin_ref, h_ref, q_scale, rope):
    if rope:
        scale = jnp.where(j == SEG_KR, DK_RET ** -0.5, 1.0).astype(F32)
        cos = cos_ref[...] * scale
        sin = sin_ref[...] * scale
        half = DK_RET // 2
        for hh in range(H_RET):
            a = hh * DK_RET
            x1 = acc[:, a:a + half]
            x2 = acc[:, a + half:a + DK_RET]
            h_ref[:, a:a + half] = (x1 * cos - x2 * sin).astype(h_ref.dtype)
            h_ref[:, a + half:a + DK_RET] = (x2 * cos + x1 * sin).astype(h_ref.dtype)
    else:
        scale = jnp.where(j == SEG_QA, q_scale, 1.0).astype(F32)
        h_ref[...] = (acc * scale).astype(h_ref.dtype)


def _in_proj_kernel(xb_ref, xs_ref, w_ref, cos_ref, sin_ref, cos_s_ref, sin_s_ref, h_ref, hs_ref, wb_ref,
                    *, q_scale, q_scale_s):
    j = pl.program_id(0)
    i = pl.program_id(1)

    @pl.when(i == 0)
    def _cast_weights():
        wb_ref[...] = w_ref[...].astype(BF16)

    def project(rope):
        acc = jnp.dot(xb_ref[...], wb_ref[...], preferred_element_type=F32)
        _store_segment(acc, j, cos_ref, sin_ref, h_ref, q_scale, rope)

        @pl.when(i == 0)
        def _sample_rows():
            acc_s = jnp.dot(xs_ref[...].astype(BF16), wb_ref[...], preferred_element_type=F32)
            _store_segment(acc_s, j, cos_s_ref, sin_s_ref, hs_ref, q_scale_s, rope)

    pl.when(j <= SEG_KR)(functools.partial(project, True))
    pl.when(j > SEG_KR)(functools.partial(project, False))


def _in_proj(xb2d, xs2d, w, cos, sin, cos_s, sin_s, rows, q_scale, q_scale_s):
    m, d = xb2d.shape
    n_s = xs2d.shape[0]
    assert w.shape[1] == (N_SEG + 2) * SEG and m % rows == 0 and cos.shape[0] % rows == 0
    n_i = m // rows
    pos_blocks = cos.shape[0] // rows
    half = DK_RET // 2
    const2 = lambda j, i: (0, 0)
    w_seg = lambda j, i: (0, jnp.where(j < W_SEG_KA, j, j + 2))
    return pl.pallas_call(
        functools.partial(_in_proj_kernel, q_scale=q_scale, q_scale_s=q_scale_s),
        grid=(N_SEG, n_i),
        in_specs=[
            pl.BlockSpec((rows, d), lambda j, i: (i, 0)),
            pl.BlockSpec((n_s, d), const2),
            pl.BlockSpec((d, SEG), w_seg),
            pl.BlockSpec((rows, half), lambda j, i: (i % pos_blocks, 0)),
            pl.BlockSpec((rows, half), lambda j, i: (i % pos_blocks, 0)),
            pl.BlockSpec((n_s, half), const2),
            pl.BlockSpec((n_s, half), const2),
        ],
        out_specs=[
            pl.BlockSpec((rows, SEG), lambda j, i: (i, j)),
            pl.BlockSpec((n_s, SEG), lambda j, i: (0, j)),
        ],
        out_shape=[
            jax.ShapeDtypeStruct((m, N_SEG * SEG), BF16),
            jax.ShapeDtypeStruct((n_s, N_SEG * SEG), F32),
        ],
        scratch_shapes=[pltpu.VMEM((d, SEG), BF16)],
        compiler_params=_params("arbitrary", "arbitrary"),
        name="in_proj",
    )(xb2d, xs2d, w, cos, sin, cos_s, sin_s)


def _retention_prompt_kernel(lg_ref, q_ref, k_ref, v_ref, z_ref, y_ref, s_out_ref,
                             s_ref, decay_ref, cross_ref, kdec_ref):
    c = pl.program_id(1)
    chunk = q_ref.shape[0]

    @pl.when(c == 0)
    def _init():
        s_ref[...] = jnp.zeros_like(s_ref)
        row = lax.broadcasted_iota(jnp.int32, (chunk, chunk), 0)
        col = lax.broadcasted_iota(jnp.int32, (chunk, chunk), 1)
        diff = (row - col).astype(F32)
        pos = lax.broadcasted_iota(jnp.int32, (chunk, DV_RET), 0).astype(F32)
        for hh in range(H_RET):
            lg = lg_ref[hh]
            decay_ref[hh] = jnp.where(diff >= 0, jnp.exp(lg * jnp.maximum(diff, 0.0)), 0.0)
            cross_ref[hh] = jnp.exp(lg * (pos + 1.0))
            kdec_ref[hh] = jnp.exp(lg * (chunk - 1.0 - pos))

    for hh in range(H_RET):
        sl = slice(hh * DK_RET, (hh + 1) * DK_RET)
        q = q_ref[:, sl]
        k = k_ref[:, sl]
        v = v_ref[:, sl]
        s_old = s_ref[hh]
        intra = (_nt_dot(q, k) * decay_ref[hh]).astype(BF16)
        o = jnp.dot(intra, v, preferred_element_type=F32)
        o = o + jnp.dot(q, s_old.astype(BF16), preferred_element_type=F32) * cross_ref[hh]
        k_dec = (k.astype(F32) * kdec_ref[hh]).astype(BF16)
        state_decay = jnp.exp(jnp.full((1, DV_RET), lg_ref[hh] * chunk, F32))
        s_new = s_old * state_decay + lax.dot_general(
            k_dec, v, (((0,), (0,)), ((), ())), preferred_element_type=F32)
        s_ref[hh] = s_new
        mu = jnp.mean(o, axis=-1, keepdims=True)
        cen = o - mu
        var = jnp.mean(cen * cen, axis=-1, keepdims=True)
        gate = _silu(z_ref[:, sl].astype(F32))
        y_ref[:, sl] = (cen * lax.rsqrt(var + GN_EPS) * gate).astype(y_ref.dtype)

    @pl.when(c == pl.num_programs(1) - 1)
    def _emit_state():
        s_out_ref[...] = s_ref[...]


def _retention_prompt(h, log_gamma, batch, seq):
    chunk = RET_CHUNK
    assert seq % chunk == 0
    nc = seq // chunk
    blk = lambda seg: pl.BlockSpec((chunk, SEG), lambda b, c, seg=seg: (b * nc + c, seg))
    return pl.pallas_call(
        _retention_prompt_kernel,
        grid=(batch, nc),
        in_specs=[pl.BlockSpec(memory_space=pltpu.SMEM),
                  blk(SEG_QR), blk(SEG_KR), blk(SEG_VR), blk(SEG_ZR)],
        out_specs=[
            pl.BlockSpec((chunk, W_RET), lambda b, c: (b * nc + c, 0)),
            pl.BlockSpec((None, H_RET, DK_RET, DV_RET), lambda b, c: (b, 0, 0, 0)),
        ],
        out_shape=[
            jax.ShapeDtypeStruct((batch * seq, W_RET), BF16),
            jax.ShapeDtypeStruct((batch, H_RET, DK_RET, DV_RET), F32),
        ],
        scratch_shapes=[
            pltpu.VMEM((H_RET, DK_RET, DV_RET), F32),
            pltpu.VMEM((H_RET, chunk, chunk), F32),
            pltpu.VMEM((H_RET, chunk, DV_RET), F32),
            pltpu.VMEM((H_RET, chunk, DK_RET), F32),
        ],
        compiler_params=_params("arbitrary", "arbitrary"),
        name="retention_prompt",
    )(log_gamma, h, h, h, h)


def _diff_lambda_in_kernel(lq1_ref, lk1_ref, lq2_ref, lk2_ref, lam_init):
    a = jnp.sum(lq1_ref[...] * lk1_ref[...], axis=-1, keepdims=True)
    b = jnp.sum(lq2_ref[...] * lk2_ref[...], axis=-1, keepdims=True)
    return jnp.exp(a) - jnp.exp(b) + lam_init


def _diff_prompt_kernel(lq1_ref, lk1_ref, lq2_ref, lk2_ref, gain_ref, q_ref, k_ref, v_ref, z_ref, y_ref,
                        sa_ref, sb_ref, ma_ref, mb_ref, m_ref, l_ref, acc_ref, *, lam_init):
    i = pl.program_id(2)
    blk = q_ref.shape[0]
    lanes = m_ref.shape[-1]
    qs = (q_ref[:, :DK_DIFF], q_ref[:, DK_DIFF:])

    m_ref[...] = jnp.full_like(m_ref, NEG_INF)
    l_ref[...] = jnp.zeros_like(l_ref)
    acc_ref[...] = jnp.zeros_like(acc_ref)

    def scores_into(bufs, j):
        buf_ref, bmax_ref = bufs
        start = pl.multiple_of(j * blk, blk)
        kb = k_ref[pl.ds(start, blk), :]
        for c in range(2):
            s = _nt_dot(qs[c], kb[:, c * DK_DIFF:(c + 1) * DK_DIFF])
            buf_ref[c] = s
            bmax_ref[c] = jnp.broadcast_to(jnp.max(s, axis=-1, keepdims=True), (blk, lanes))

    def softmax_pv(bufs, j, masked):
        buf_ref, bmax_ref = bufs
        start = pl.multiple_of(j * blk, blk)
        vb = v_ref[pl.ds(start, blk), :]
        if masked:
            row = lax.broadcasted_iota(jnp.int32, (blk, blk), 0)
            col = lax.broadcasted_iota(jnp.int32, (blk, blk), 1)
            keep = col <= row
        for c in range(2):
            s = buf_ref[c]
            m_old = m_ref[c]
            if masked:
                s = jnp.where(keep, s, NEG_INF)
                m_new = jnp.maximum(m_old, jnp.max(s, axis=-1, keepdims=True))
            else:
                m_new = jnp.maximum(m_old, bmax_ref[c])
            alpha = jnp.exp2(m_old - m_new)
            p = jnp.exp2(s - jnp.concatenate([m_new] * (blk // lanes), axis=1))
            part = p[:, 0:lanes]
            for t in range(1, blk // lanes):
                part = part + p[:, t * lanes:(t + 1) * lanes]
            l_ref[c] = alpha * l_ref[c] + part
            acc_ref[c] = (jnp.concatenate([alpha] * (DV_DIFF // lanes), axis=1) * acc_ref[c]
                          + jnp.dot(p.astype(BF16), vb, preferred_element_type=F32))
            m_ref[c] = m_new

    buf_a = (sa_ref, ma_ref)
    buf_b = (sb_ref, mb_ref)
    scores_into(buf_a, 0)

    def pair(t, carry):
        j = 2 * t
        scores_into(buf_b, j + 1)
        softmax_pv(buf_a, j, False)
        scores_into(buf_a, j + 2)
        softmax_pv(buf_b, j + 1, False)
        return carry

    lax.fori_loop(0, i // 2, pair, 0)

    @pl.when(i % 2 == 1)
    def _odd_tail():
        scores_into(buf_b, i)
        softmax_pv(buf_a, i - 1, False)
        softmax_pv(buf_b, i, True)

    @pl.when(i % 2 == 0)
    def _even_tail():
        softmax_pv(buf_a, i, True)

    lam = _diff_lambda_in_kernel(lq1_ref, lk1_ref, lq2_ref, lk2_ref, lam_init)
    l0 = jnp.sum(l_ref[0], axis=-1, keepdims=True)
    l1 = jnp.sum(l_ref[1], axis=-1, keepdims=True)
    o = acc_ref[0] / l0 - lam * (acc_ref[1] / l1)
    o = o * lax.rsqrt(jnp.mean(o * o, axis=-1, keepdims=True) + GN_EPS)
    o = o * gain_ref[...] * (1.0 - lam_init)
    y_ref[...] = (o * _silu(z_ref[...].astype(F32))).astype(y_ref.dtype)


def _diff_prompt(h, hkv, lam_params, gain, lam_init, batch, seq):
    blk = Q_BLOCK
    assert seq % blk == 0
    nq = seq // blk
    heads_per_seg = SEG // DV_DIFF
    small = pl.BlockSpec((1, DK_DIFF), lambda b, hh, i: (0, 0))
    return pl.pallas_call(
        functools.partial(_diff_prompt_kernel, lam_init=lam_init),
        grid=(batch, H_DIFF, nq),
        in_specs=[
            small, small, small, small,
            pl.BlockSpec((1, DV_DIFF), lambda b, hh, i: (0, 0)),
            pl.BlockSpec((blk, 2 * DK_DIFF), lambda b, hh, i: (b * nq + i, SEG_QA * heads_per_seg + hh)),
            pl.BlockSpec((seq, 2 * DK_DIFF), lambda b, hh, i: (b, hh)),
            pl.BlockSpec((seq, DV_DIFF), lambda b, hh, i: (b, heads_per_seg + hh)),
            pl.BlockSpec((blk, DV_DIFF), lambda b, hh, i: (b * nq + i, SEG_ZA * heads_per_seg + hh)),
        ],
        out_specs=pl.BlockSpec((blk, DV_DIFF), lambda b, hh, i: (b * nq + i, hh)),
        out_shape=jax.ShapeDtypeStruct((batch * seq, W_DIFF), BF16),
        scratch_shapes=[
            pltpu.VMEM((2, blk, blk), F32),
            pltpu.VMEM((2, blk, blk), F32),
            pltpu.VMEM((2, blk, LANES), F32),
            pltpu.VMEM((2, blk, LANES), F32),
            pltpu.VMEM((2, blk, LANES), F32),
            pltpu.VMEM((2, blk, LANES), F32),
            pltpu.VMEM((2, blk, DV_DIFF), F32),
        ],
        compiler_params=_params("arbitrary", "arbitrary", "arbitrary"),
        name="diff_prompt",
    )(*lam_params, gain, h, hkv, hkv, h)


def _retention_sample_kernel(lg_ref, q_ref, k_ref, kt_ref, v_ref, vall_ref, z_ref, s_in_ref,
                             y_ref, s_out_ref):
    g = pl.program_id(0)
    hh = pl.program_id(1)
    group = q_ref.shape[0]
    n_seq = kt_ref.shape[1]
    gamma = jnp.exp(jnp.full((1, DV_RET), lg_ref[hh], F32))
    q_bf = q_ref[...].astype(BF16)
    k_bf = k_ref[...].astype(BF16)
    v_bf = v_ref[...].astype(BF16)
    qk = jnp.sum(q_bf.astype(F32) * k_bf.astype(F32), axis=-1, keepdims=True)
    kt_bf = kt_ref[...].astype(BF16)
    vall_bf = vall_ref[...].astype(BF16)
    seq_lane = lax.broadcasted_iota(jnp.int32, (DK_RET, n_seq), 1)
    row = lax.broadcasted_iota(jnp.int32, (group, DV_RET), 0)
    cross = jnp.zeros((group, DV_RET), F32)
    for s in range(group):
        s_old = s_in_ref[s]
        qs = jnp.dot(q_bf, s_old.astype(BF16), preferred_element_type=F32)
        cross = jnp.where(row == s, qs, cross)
        k_only = jnp.where(seq_lane == g * group + s, kt_bf, jnp.zeros_like(kt_bf))
        outer = jnp.dot(k_only, vall_bf, preferred_element_type=F32)
        s_out_ref[s] = s_old * gamma + outer
    o = qk * v_bf.astype(F32) + cross * gamma
    mu = jnp.mean(o, axis=-1, keepdims=True)
    cen = o - mu
    var = jnp.mean(cen * cen, axis=-1, keepdims=True)
    y_ref[...] = (cen * lax.rsqrt(var + GN_EPS) * _silu(z_ref[...].astype(F32))).astype(y_ref.dtype)


def _retention_sample(hs, kt, state, log_gamma):
    n_seq = hs.shape[0]
    group = SAMPLE_GROUP
    assert n_seq % group == 0
    per_seg = SEG // DK_RET
    col = lambda seg: pl.BlockSpec((group, DK_RET), lambda g, hh, seg=seg: (g, seg * per_seg + hh))
    st = pl.BlockSpec((group, None, DK_RET, DV_RET), lambda g, hh: (g, hh, 0, 0))
    return pl.pallas_call(
        _retention_sample_kernel,
        grid=(n_seq // group, H_RET),
        in_specs=[
            pl.BlockSpec(memory_space=pltpu.SMEM),
            col(SEG_QR), col(SEG_KR),
            pl.BlockSpec((DK_RET, n_seq), lambda g, hh: (hh, 0)),
            col(SEG_VR),
            pl.BlockSpec((n_seq, DV_RET), lambda g, hh: (0, SEG_VR * per_seg + hh)),
            col(SEG_ZR),
            st,
        ],
        out_specs=[pl.BlockSpec((group, DV_RET), lambda g, hh: (g, hh)), st],
        out_shape=[
            jax.ShapeDtypeStruct((n_seq, W_RET), F32),
            jax.ShapeDtypeStruct(state.shape, F32),
        ],
        compiler_params=_params("arbitrary", "arbitrary"),
        name="retention_sample",
    )(log_gamma, hs, hs, kt, hs, hs, hs, state)


def _diff_sample_kernel(pt_ref, lq1_ref, lk1_ref, lq2_ref, lk2_ref, gain_ref, q_ref, kn_ref, vn_ref, z_ref,
                        ck_hbm, cv_hbm, y_ref, kbuf_ref, vbuf_ref, s_ref, bias_ref, ksem_ref, vsem_ref,
                        *, n_seq, layer, lam_init):
    b = pl.program_id(0)
    n_slots, n_pages, page = kbuf_ref.shape[0], kbuf_ref.shape[1], kbuf_ref.shape[2]
    rows = page * H_DIFF
    lanes = bias_ref.shape[-1]

    def page_copies(seq, slot):
        out = []
        for p in range(n_pages):
            src = pt_ref[seq, p]
            out.append(pltpu.make_async_copy(ck_hbm.at[layer, src], kbuf_ref.at[slot, p], ksem_ref.at[slot]))
            out.append(pltpu.make_async_copy(cv_hbm.at[layer, src], vbuf_ref.at[slot, p], vsem_ref.at[slot]))
        return out

    def request(seq):
        for cp in page_copies(seq, seq % n_slots):
            cp.start()

    @pl.when(b == 0)
    def _first_requests():
        for seq in range(min(n_slots - 1, n_seq)):
            request(seq)

    @pl.when(b + (n_slots - 1) < n_seq)
    def _request_ahead():
        request(b + (n_slots - 1))

    slot = b % n_slots
    for cp in page_copies(b, slot):
        cp.wait()
    k_pages = [kbuf_ref.at[slot, p] for p in range(n_pages)]
    v_pages = [vbuf_ref.at[slot, p] for p in range(n_pages)]

    @pl.when(b == 0)
    def _init_bias():
        r = lax.broadcasted_iota(jnp.int32, (rows, lanes), 0)
        j = lax.broadcasted_iota(jnp.int32, (rows, lanes), 1)
        bias_ref[...] = jnp.where((j < 2 * H_DIFF) & (r % H_DIFF == j % H_DIFF), 0.0, NEG_INF)

    q4 = q_ref[...]
    in_map0 = lax.broadcasted_iota(jnp.int32, q4.shape, 1) < DK_DIFF
    q_cols = jnp.concatenate(
        [jnp.where(in_map0, q4, 0.0), jnp.where(in_map0, 0.0, q4),
         jnp.zeros((lanes - 2 * H_DIFF, 2 * DK_DIFF), F32)], axis=0).astype(BF16)

    kn2 = jnp.concatenate([kn_ref[...], kn_ref[...]], axis=0)
    new_rows = kn2.shape[0]
    new_bias = jnp.where(lax.broadcasted_iota(jnp.int32, (new_rows, lanes), 0) < H_DIFF,
                         bias_ref[0:new_rows, :], NEG_INF)
    s_new = _nt_dot(kn2.astype(BF16), q_cols) + new_bias
    m = jnp.max(s_new, axis=0, keepdims=True)
    for p in range(n_pages):
        k_mat = k_pages[p][...].reshape(rows, 2 * DK_DIFF).astype(BF16)
        s = _nt_dot(k_mat, q_cols) + bias_ref[...]
        s_ref[p] = s
        m = jnp.maximum(m, jnp.max(s, axis=0, keepdims=True))
    e_new = jnp.exp(s_new - m)
    l = jnp.sum(e_new, axis=0, keepdims=True)
    for p in range(n_pages):
        e = jnp.exp(s_ref[p] - m)
        s_ref[p] = e
        l = l + jnp.sum(e, axis=0, keepdims=True)
    lam = _diff_lambda_in_kernel(lq1_ref, lk1_ref, lq2_ref, lk2_ref, lam_init)
    lane = lax.broadcasted_iota(jnp.int32, (1, lanes), 1)
    coef = jnp.where(lane < H_DIFF, 1.0, -lam) / l
    coef = jnp.where(lane < 2 * H_DIFF, coef, 0.0)
    spread = jnp.ones((lanes, DV_DIFF), BF16)
    a_new = jnp.dot((e_new * coef).astype(BF16), spread, preferred_element_type=F32)
    acc = a_new * jnp.concatenate([vn_ref[...], vn_ref[...]], axis=0)
    for p in range(n_pages):
        a = jnp.dot((s_ref[p] * coef).astype(BF16), spread, preferred_element_type=F32)
        av = a * v_pages[p][...].reshape(rows, DV_DIFF)
        acc = acc + jnp.sum(av.reshape(rows // new_rows, new_rows, DV_DIFF), axis=0)
    o = acc[0:H_DIFF, :] + acc[H_DIFF:new_rows, :]
    o = o * lax.rsqrt(jnp.mean(o * o, axis=-1, keepdims=True) + GN_EPS)
    o = o * gain_ref[...] * (1.0 - lam_init)
    y_ref[...] = (o * _silu(z_ref[...])).astype(y_ref.dtype)


def _diff_sample(q4, kn4, vn4, z4, cache_k, cache_v, layer, page_table, lam_params, gain, lam_init):
    n_seq = q4.shape[0]
    n_pages = page_table.shape[1]
    page = cache_k.shape[2]
    lanes = 128
    small = pl.BlockSpec((1, DK_DIFF), lambda b, pt: (0, 0))
    tok = pl.BlockSpec((None, H_DIFF, DV_DIFF), lambda b, pt: (b, 0, 0))
    in_hbm = pl.BlockSpec(memory_space=pl.ANY)
    page_slots = pltpu.VMEM((PAGE_SLOTS, n_pages, page, H_DIFF, DV_DIFF), F32)
    grid_spec = pltpu.PrefetchScalarGridSpec(
        num_scalar_prefetch=1,
        grid=(n_seq,),
        in_specs=[small, small, small, small,
                  pl.BlockSpec((1, DV_DIFF), lambda b, pt: (0, 0)),
                  tok, tok, tok, tok, in_hbm, in_hbm],
        out_specs=tok,
        scratch_shapes=[page_slots, page_slots,
                        pltpu.VMEM((n_pages, page * H_DIFF, lanes), F32),
                        pltpu.VMEM((page * H_DIFF, lanes), F32),
                        pltpu.SemaphoreType.DMA((PAGE_SLOTS,)),
                        pltpu.SemaphoreType.DMA((PAGE_SLOTS,))],
    )
    return pl.pallas_call(
        functools.partial(_diff_sample_kernel, n_seq=n_seq, layer=layer, lam_init=lam_init),
        grid_spec=grid_spec,
        out_shape=jax.ShapeDtypeStruct((n_seq, H_DIFF, DV_DIFF), F32),
        compiler_params=pltpu.CompilerParams(dimension_semantics=("arbitrary",),
                                             vmem_limit_bytes=DECODE_VMEM_LIMIT_BYTES),
        name="diff_sample",
    )(page_table, *lam_params, gain, q4, kn4, vn4, z4, cache_k, cache_v)


def _merge_rows(x_ref, yr_ref, ya_ref, gr_ref, ga_ref, wr_ref, wa_ref, wo_ref, lng_ref, lnb_ref, y_ref, alpha):
    rows = x_ref.shape[0]
    part = max(rows // MERGE_PARTS, min(rows, MERGE_MIN_PART))
    for t in range(rows // part):
        rs = slice(t * part, (t + 1) * part)
        br = jnp.dot(yr_ref[rs, :].astype(BF16), wr_ref[...], preferred_element_type=F32)
        ba = jnp.dot(ya_ref[rs, :].astype(BF16), wa_ref[...], preferred_element_type=F32)
        m = _sigmoid(gr_ref[rs, :].astype(F32)) * br + _sigmoid(ga_ref[rs, :].astype(F32)) * ba
        h = alpha * x_ref[rs, :] + jnp.dot(m.astype(BF16), wo_ref[...], preferred_element_type=F32)
        mu = jnp.mean(h, axis=-1, keepdims=True)
        cen = h - mu
        var = jnp.mean(cen * cen, axis=-1, keepdims=True)
        y_ref[rs, :] = cen * lax.rsqrt(var + LN_EPS) * lng_ref[...] + lnb_ref[...]


def _merge_kernel(x_ref, yr_ref, ya_ref, gr_ref, ga_ref, xs_ref, yrs_ref, yas_ref, grs_ref, gas_ref,
                  wr_ref, wa_ref, wo_ref, lng_ref, lnb_ref, y_ref, ys_ref, *, alpha):
    weights = (wr_ref, wa_ref, wo_ref, lng_ref, lnb_ref)
    _merge_rows(x_ref, yr_ref, ya_ref, gr_ref, ga_ref, *weights, y_ref, alpha)

    @pl.when(pl.program_id(0) == 0)
    def _sample_rows():
        _merge_rows(xs_ref, yrs_ref, yas_ref, grs_ref, gas_ref, *weights, ys_ref, alpha)


def _merge(x2d, y_r, y_a, h, xs2d, y_rs, y_as, hs, w_r, w_a, w_o, ln_gain, ln_bias, rows, alpha):
    m, d = x2d.shape
    n_s = xs2d.shape[0]
    assert m % rows == 0
    gate_blocks = (2 * SEG) // d if d <= 2 * SEG else None
    assert gate_blocks == 1, "gates are two segments wide, same as the model width"
    const = lambda shape, col=0: pl.BlockSpec(shape, lambda i: (0, col), pipeline_mode=pl.Buffered(1))
    return pl.pallas_call(
        functools.partial(_merge_kernel, alpha=alpha),
        grid=(m // rows,),
        in_specs=[
            pl.BlockSpec((rows, d), lambda i: (i, 0)),
            pl.BlockSpec((rows, W_RET), lambda i: (i, 0)),
            pl.BlockSpec((rows, W_DIFF), lambda i: (i, 0)),
            pl.BlockSpec((rows, d), lambda i: (i, SEG_GR * SEG // d)),
            pl.BlockSpec((rows, d), lambda i: (i, SEG_GA * SEG // d)),
            const((n_s, d)), const((n_s, W_RET)), const((n_s, W_DIFF)),
            const((n_s, d), SEG_GR * SEG // d), const((n_s, d), SEG_GA * SEG // d),
            const((W_RET, d)), const((W_DIFF, d)), const((d, d)),
            const((1, d)), const((1, d)),
        ],
        out_specs=[pl.BlockSpec((rows, d), lambda i: (i, 0)),
                   pl.BlockSpec((n_s, d), lambda i: (0, 0))],
        out_shape=[jax.ShapeDtypeStruct((m, d), F32), jax.ShapeDtypeStruct((n_s, d), F32)],
        compiler_params=_params("arbitrary"),
        name="merge",
    )(x2d, y_r, y_a, h, h, xs2d, y_rs, y_as, hs, hs, w_r, w_a, w_o, ln_gain, ln_bias)


def _rope_tables(pos):
    half = DK_RET // 2
    inv_freq = 1.0 / (ROPE_BASE ** (np.arange(half, dtype=np.float64) / half))
    ang = np.asarray(pos, np.float64)[:, None] * inv_freq[None, :]
    return jnp.asarray(np.cos(ang), F32), jnp.asarray(np.sin(ang), F32)


def kernel(x_prompt, x_sample, state_ret, cache_k, cache_v, page_table, w_in, w_branch_ret, w_branch_diff, w_out, lambda_q1, lambda_k1, lambda_q2, lambda_k2, subln_gain, ln_gain, ln_bias):
    batch, seq, d_model = x_prompt.shape
    n_seq, t_s, _ = x_sample.shape
    depth = w_in.shape[0]
    assert t_s == 1, "the sample kernels implement the one-new-token step"
    n_pool, page = cache_k.shape[1], cache_k.shape[2]
    past_len = page_table.shape[1] * page
    alpha = (2.0 * depth) ** 0.25

    log_gamma = jnp.asarray(np.log(1.0 - np.exp2(-5.0 - np.arange(H_RET, dtype=np.float64))), F32)
    cos_p, sin_p = _rope_tables(np.arange(seq))
    cos_s, sin_s = _rope_tables(np.full((n_seq,), past_len))

    hp = x_prompt.reshape(batch * seq, d_model)
    hs = x_sample.reshape(n_seq, d_model)
    outs = [[] for _ in range(6)]
    for l in range(depth):
        lam_init = 0.8 - 0.6 * math.exp(-0.3 * l)
        lam_params = [p[l].reshape(1, DK_DIFF) for p in (lambda_q1, lambda_k1, lambda_q2, lambda_k2)]
        gain = subln_gain[l].reshape(1, DV_DIFF)
        lng = ln_gain[l].reshape(1, d_model)
        lnb = ln_bias[l].reshape(1, d_model)

        hkv_p, k_p, v_p, hp_bf, k_s, v_s, w_r_bf, w_a_bf, w_o_bf = _kv_proj(
            hp, hs, w_in[l], [w_branch_ret[l], w_branch_diff[l], w_out[l]], KV_PROJ_ROWS)
        h_p, h_s = _in_proj(hp_bf, hs, w_in[l], cos_p, sin_p, cos_s, sin_s, PROJ_ROWS,
                            DK_DIFF ** -0.5 * LOG2E, DK_DIFF ** -0.5)

        y_r, s_p = _retention_prompt(h_p, log_gamma, batch, seq)
        y_a = _diff_prompt(h_p, hkv_p, lam_params, gain, lam_init, batch, seq)

        kt = h_s[:, SEG_KR * SEG:(SEG_KR + 1) * SEG].T
        y_rs, s_s = _retention_sample(h_s, kt, state_ret[l], log_gamma)
        tok = lambda seg: h_s[:, seg * SEG:(seg + 1) * SEG].reshape(n_seq, H_DIFF, DV_DIFF)
        y_as = _diff_sample(tok(SEG_QA), k_s, v_s, tok(SEG_ZA), cache_k, cache_v, l,
                            page_table, lam_params, gain, lam_init)

        hp, hs = _merge(hp, y_r, y_a, h_p, hs, y_rs, y_as.reshape(n_seq, W_DIFF), h_s,
                        w_r_bf, w_a_bf, w_o_bf, lng, lnb, MERGE_ROWS, alpha)

        outs[0].append(s_p)
        outs[1].append(k_p.reshape(batch, seq, H_DIFF, 2 * DK_DIFF))
        outs[2].append(v_p.reshape(batch, seq, H_DIFF, DV_DIFF))
        outs[3].append(s_s)
        outs[4].append(k_s.reshape(n_seq, t_s, H_DIFF, 2 * DK_DIFF))
        outs[5].append(v_s.reshape(n_seq, t_s, H_DIFF, DV_DIFF))

    return (hp.reshape(batch, seq, d_model), hs.reshape(n_seq, t_s, d_model),
            *[jnp.stack(o) for o in outs])
```

```python
import functools
import math

import jax
import jax.numpy as jnp
import numpy as np
from jax import lax
from jax.experimental import pallas as pl
from jax.experimental.pallas import tpu as pltpu

F32 = jnp.float32
BF16 = jnp.bfloat16

H_RET = 4
DK_RET = 256
DV_RET = 256
H_DIFF = 4
DK_DIFF = 128
DV_DIFF = 256
ROPE_BASE = 10000.0
LN_EPS = 1e-5
GN_EPS = 1e-6
NEG_INF = -1e30

SEG = 1024
W_RET = H_RET * DV_RET
W_DIFF = H_DIFF * DV_DIFF
W_SEG_KA = 5
SEG_QR, SEG_KR, SEG_VR, SEG_ZR, SEG_QA, SEG_ZA = range(6)
SEG_GR = 6
SEG_GA = 8
N_SEG = 10

LANES = 128
LOG2E = math.log2(math.e)
VMEM_LIMIT_BYTES = 56 * 1024 * 1024

RET_CHUNK = 256
Q_BLOCK = 1024
KV_PROJ_ROWS = 256
PROJ_ROWS = 1024
MERGE_ROWS = 512
MERGE_PARTS = 2
MERGE_MIN_PART = 256
SAMPLE_GROUP = 32
PAGE_SLOTS = 3
DECODE_VMEM_LIMIT_BYTES = 60 * 1024 * 1024


def _params(*sem):
    return pltpu.CompilerParams(dimension_semantics=sem, vmem_limit_bytes=VMEM_LIMIT_BYTES)


def _silu(z):
    return z * (1.0 / (1.0 + jnp.exp(-z)))


def _sigmoid(z):
    return 1.0 / (1.0 + jnp.exp(-z))


def _nt_dot(a, b):
    return lax.dot_general(a, b, (((1,), (1,)), ((), ())), preferred_element_type=F32)


def _kv_proj_kernel(x_ref, xs_ref, wk_ref, wv_ref, hkv_ref, k_ref, v_ref, xb_ref, ks_ref, vs_ref, wkb_ref, wvb_ref,
                    *, side_refs, side_steps):
    i = pl.program_id(0)

    @pl.when(i == 0)
    def _cast_weights():
        wkb_ref[...] = wk_ref[...].astype(BF16)
        wvb_ref[...] = wv_ref[...].astype(BF16)

    xb = x_ref[...].astype(BF16)
    xb_ref[...] = xb
    acc_k = jnp.dot(xb, wkb_ref[...], preferred_element_type=F32)
    hkv_ref[:, :SEG] = acc_k.astype(hkv_ref.dtype)
    k_ref[...] = acc_k.reshape(k_ref.shape)
    acc_v = jnp.dot(xb, wvb_ref[...], preferred_element_type=F32)
    hkv_ref[:, SEG:] = acc_v.astype(hkv_ref.dtype)
    v_ref[...] = acc_v.reshape(v_ref.shape)

    @pl.when(i == 0)
    def _sample_rows():
        xs = xs_ref[...].astype(BF16)
        ks_ref[...] = jnp.dot(xs, wkb_ref[...], preferred_element_type=F32).reshape(ks_ref.shape)
        vs_ref[...] = jnp.dot(xs, wvb_ref[...], preferred_element_type=F32).reshape(vs_ref.shape)

    n_side = len(side_refs) // 2
    for t in range(n_side):
        first, count = side_steps[t]

        @pl.when((i >= first) & (i < first + count))
        def _cast_side(t=t):
            side_refs[n_side + t][...] = side_refs[t][...].astype(BF16)


def _kv_proj(x2d, xs2d, w, side_weights, rows):
    m, d = x2d.shape
    n_s = xs2d.shape[0]
    assert m % rows == 0
    n_i = m // rows
    side_rows = [sw.shape[0] for sw in side_weights]
    chunk = sum(side_rows) // n_i
    assert chunk * n_i == sum(side_rows) and chunk % 16 == 0 and all(r % chunk == 0 for r in side_rows)
    side_steps, first = [], 0
    for r in side_rows:
        side_steps.append((first, r // chunk))
        first += r // chunk

    def side_spec(t):
        first, count = side_steps[t]
        return pl.BlockSpec((chunk, side_weights[t].shape[1]),
                            lambda i: (jnp.clip(i - first, 0, count - 1), 0))

    side_specs = [side_spec(t) for t in range(len(side_weights))]
    resident = lambda shape, idx: pl.BlockSpec(shape, lambda i: idx, pipeline_mode=pl.Buffered(1))

    def body(x_ref, xs_ref, wk_ref, wv_ref, *rest):
        n_side = len(side_weights)
        side_in = rest[:n_side]
        hkv_ref, k_ref, v_ref, xb_ref, ks_ref, vs_ref = rest[n_side:n_side + 6]
        side_out = rest[n_side + 6:2 * n_side + 6]
        wkb_ref, wvb_ref = rest[2 * n_side + 6:]
        _kv_proj_kernel(x_ref, xs_ref, wk_ref, wv_ref, hkv_ref, k_ref, v_ref, xb_ref, ks_ref, vs_ref,
                        wkb_ref, wvb_ref, side_refs=tuple(side_in) + tuple(side_out), side_steps=tuple(side_steps))

    return pl.pallas_call(
        body,
        grid=(n_i,),
        in_specs=[
            pl.BlockSpec((rows, d), lambda i: (i, 0)),
            resident((n_s, d), (0, 0)),
            resident((d, SEG), (0, W_SEG_KA)),
            resident((d, SEG), (0, W_SEG_KA + 1)),
        ] + side_specs,
        out_specs=[
            pl.BlockSpec((rows, 2 * SEG), lambda i: (i, 0)),
            pl.BlockSpec((rows, H_DIFF, 2 * DK_DIFF), lambda i: (i, 0, 0)),
            pl.BlockSpec((rows, H_DIFF, DV_DIFF), lambda i: (i, 0, 0)),
            pl.BlockSpec((rows, d), lambda i: (i, 0)),
            pl.BlockSpec((n_s, H_DIFF, 2 * DK_DIFF), lambda i: (0, 0, 0)),
            pl.BlockSpec((n_s, H_DIFF, DV_DIFF), lambda i: (0, 0, 0)),
        ] + side_specs,
        out_shape=[
            jax.ShapeDtypeStruct((m, 2 * SEG), BF16),
            jax.ShapeDtypeStruct((m, H_DIFF, 2 * DK_DIFF), F32),
            jax.ShapeDtypeStruct((m, H_DIFF, DV_DIFF), F32),
            jax.ShapeDtypeStruct((m, d), BF16),
            jax.ShapeDtypeStruct((n_s, H_DIFF, 2 * DK_DIFF), F32),
            jax.ShapeDtypeStruct((n_s, H_DIFF, DV_DIFF), F32),
        ] + [jax.ShapeDtypeStruct(sw.shape, BF16) for sw in side_weights],
        scratch_shapes=[pltpu.VMEM((d, SEG), BF16), pltpu.VMEM((d, SEG), BF16)],
        compiler_params=_params("arbitrary"),
        name="kv_proj",
    )(x2d, xs2d, w, w, *side_weights)


def _store_segment(acc, j, cos_ref, sin_ref, h_ref, q_scale, rope):
    if rope:
        scale = jnp.where(j == SEG_KR, DK_RET ** -0.5, 1.0).astype(F32)
        cos = cos_ref[...] * scale
        sin = sin_ref[...] * scale
        half = DK_RET // 2
        for hh in range(H_RET):
            a = hh * DK_RET
            x1 = acc[:, a:a + half]
            x2 = acc[:, a + half:a + DK_RET]
            h_ref[:, a:a + half] = (x1 * cos - x2 * sin).astype(h_ref.dtype)
            h_ref[:, a + half:a + DK_RET] = (x2 * cos + x1 * sin).astype(h_ref.dtype)
    else:
        scale = jnp.where(j == SEG_QA, q_scale, 1.0).astype(F32)
        h_ref[...] = (acc * scale).astype(h_ref.dtype)


def _in_proj_kernel(xb_ref, xs_ref, w_ref, cos_ref, sin_ref, cos_s_ref, sin_s_ref, h_ref, hs_ref, wb_ref,
                    *, q_scale, q_scale_s):
    j = pl.program_id(0)
    i = pl.program_id(1)

    @pl.when(i == 0)
    def _cast_weights():
        wb_ref[...] = w_ref[...].astype(BF16)

    def project(rope):
        acc = jnp.dot(xb_ref[...], wb_ref[...], preferred_element_type=F32)
        _store_segment(acc, j, cos_ref, sin_ref, h_ref, q_scale, rope)

        @pl.when(i == 0)
        def _sample_rows():
            acc_s = jnp.dot(xs_ref[...].astype(BF16), wb_ref[...], preferred_element_type=F32)
            _store_segment(acc_s, j, cos_s_ref, sin_s_ref, hs_ref, q_scale_s, rope)

    pl.when(j <= SEG_KR)(functools.partial(project, True))
    pl.when(j > SEG_KR)(functools.partial(project, False))


def _in_proj(xb2d, xs2d, w, cos, sin, cos_s, sin_s, rows, q_scale, q_scale_s):
    m, d = xb2d.shape
    n_s = xs2d.shape[0]
    assert w.shape[1] == (N_SEG + 2) * SEG and m % rows == 0 and cos.shape[0] % rows == 0
    n_i = m // rows
    pos_blocks = cos.shape[0] // rows
    half = DK_RET // 2
    const2 = lambda j, i: (0, 0)
    w_seg = lambda j, i: (0, jnp.where(j < W_SEG_KA, j, j + 2))
    return pl.pallas_call(
        functools.partial(_in_proj_kernel, q_scale=q_scale, q_scale_s=q_scale_s),
        grid=(N_SEG, n_i),
        in_specs=[
            pl.BlockSpec((rows, d), lambda j, i: (i, 0)),
            pl.BlockSpec((n_s, d), const2),
            pl.BlockSpec((d, SEG), w_seg),
            pl.BlockSpec((rows, half), lambda j, i: (i % pos_blocks, 0)),
            pl.BlockSpec((rows, half), lambda j, i: (i % pos_blocks, 0)),
            pl.BlockSpec((n_s, half), const2),
            pl.BlockSpec((n_s, half), const2),
        ],
        out_specs=[
            pl.BlockSpec((rows, SEG), lambda j, i: (i, j)),
            pl.BlockSpec((n_s, SEG), lambda j, i: (0, j)),
        ],
        out_shape=[
            jax.ShapeDtypeStruct((m, N_SEG * SEG), BF16),
            jax.ShapeDtypeStruct((n_s, N_SEG * SEG), F32),
        ],
        scratch_shapes=[pltpu.VMEM((d, SEG), BF16)],
        compiler_params=_params("arbitrary", "arbitrary"),
        name="in_proj",
    )(xb2d, xs2d, w, cos, sin, cos_s, sin_s)


def _retention_prompt_kernel(lg_ref, q_ref, k_ref, v_ref, z_ref, y_ref, s_out_ref,
                             s_ref, decay_ref, cross_ref, kdec_ref):
    c = pl.program_id(1)
    chunk = q_ref.shape[0]

    @pl.when(c == 0)
    def _init():
        s_ref[...] = jnp.zeros_like(s_ref)
        row = lax.broadcasted_iota(jnp.int32, (chunk, chunk), 0)
        col = lax.broadcasted_iota(jnp.int32, (chunk, chunk), 1)
        diff = (row - col).astype(F32)
        pos = lax.broadcasted_iota(jnp.int32, (chunk, DV_RET), 0).astype(F32)
        for hh in range(H_RET):
            lg = lg_ref[hh]
            decay_ref[hh] = jnp.where(diff >= 0, jnp.exp(lg * jnp.maximum(diff, 0.0)), 0.0)
            cross_ref[hh] = jnp.exp(lg * (pos + 1.0))
            kdec_ref[hh] = jnp.exp(lg * (chunk - 1.0 - pos))

    for hh in range(H_RET):
        sl = slice(hh * DK_RET, (hh + 1) * DK_RET)
        q = q_ref[:, sl]
        k = k_ref[:, sl]
        v = v_ref[:, sl]
        s_old = s_ref[hh]
        intra = (_nt_dot(q, k) * decay_ref[hh]).astype(BF16)
        o = jnp.dot(intra, v, preferred_element_type=F32)
        o = o + jnp.dot(q, s_old.astype(BF16), preferred_element_type=F32) * cross_ref[hh]
        k_dec = (k.astype(F32) * kdec_ref[hh]).astype(BF16)
        state_decay = jnp.exp(jnp.full((1, DV_RET), lg_ref[hh] * chunk, F32))
        s_new = s_old * state_decay + lax.dot_general(
            k_dec, v, (((0,), (0,)), ((), ())), preferred_element_type=F32)
        s_ref[hh] = s_new
        mu = jnp.mean(o, axis=-1, keepdims=True)
        cen = o - mu
        var = jnp.mean(cen * cen, axis=-1, keepdims=True)
        gate = _silu(z_ref[:, sl].astype(F32))
        y_ref[:, sl] = (cen * lax.rsqrt(var + GN_EPS) * gate).astype(y_ref.dtype)

    @pl.when(c == pl.num_programs(1) - 1)
    def _emit_state():
        s_out_ref[...] = s_ref[...]


def _retention_prompt(h, log_gamma, batch, seq):
    chunk = RET_CHUNK
    assert seq % chunk == 0
    nc = seq // chunk
    blk = lambda seg: pl.BlockSpec((chunk, SEG), lambda b, c, seg=seg: (b * nc + c, seg))
    return pl.pallas_call(
        _retention_prompt_kernel,
        grid=(batch, nc),
        in_specs=[pl.BlockSpec(memory_space=pltpu.SMEM),
                  blk(SEG_QR), blk(SEG_KR), blk(SEG_VR), blk(SEG_ZR)],
        out_specs=[
            pl.BlockSpec((chunk, W_RET), lambda b, c: (b * nc + c, 0)),
            pl.BlockSpec((None, H_RET, DK_RET, DV_RET), lambda b, c: (b, 0, 0, 0)),
        ],
        out_shape=[
            jax.ShapeDtypeStruct((batch * seq, W_RET), BF16),
            jax.ShapeDtypeStruct((batch, H_RET, DK_RET, DV_RET), F32),
        ],
        scratch_shapes=[
            pltpu.VMEM((H_RET, DK_RET, DV_RET), F32),
            pltpu.VMEM((H_RET, chunk, chunk), F32),
            pltpu.VMEM((H_RET, chunk, DV_RET), F32),
            pltpu.VMEM((H_RET, chunk, DK_RET), F32),
        ],
        compiler_params=_params("arbitrary", "arbitrary"),
        name="retention_prompt",
    )(log_gamma, h, h, h, h)


def _diff_lambda_in_kernel(lq1_ref, lk1_ref, lq2_ref, lk2_ref, lam_init):
    a = jnp.sum(lq1_ref[...] * lk1_ref[...], axis=-1, keepdims=True)
    b = jnp.sum(lq2_ref[...] * lk2_ref[...], axis=-1, keepdims=True)
    return jnp.exp(a) - jnp.exp(b) + lam_init


def _diff_prompt_kernel(lq1_ref, lk1_ref, lq2_ref, lk2_ref, gain_ref, q_ref, k_ref, v_ref, z_ref, y_ref,
                        sa_ref, sb_ref, ma_ref, mb_ref, m_ref, l_ref, acc_ref, *, lam_init):
    i = pl.program_id(2)
    blk = q_ref.shape[0]
    lanes = m_ref.shape[-1]
    qs = (q_ref[:, :DK_DIFF], q_ref[:, DK_DIFF:])

    m_ref[...] = jnp.full_like(m_ref, NEG_INF)
    l_ref[...] = jnp.zeros_like(l_ref)
    acc_ref[...] = jnp.zeros_like(acc_ref)

    def scores_into(bufs, j):
        buf_ref, bmax_ref = bufs
        start = pl.multiple_of(j * blk, blk)
        kb = k_ref[pl.ds(start, blk), :]
        for c in range(2):
            s = _nt_dot(qs[c], kb[:, c * DK_DIFF:(c + 1) * DK_DIFF])
            buf_ref[c] = s
            bmax_ref[c] = jnp.broadcast_to(jnp.max(s, axis=-1, keepdims=True), (blk, lanes))

    def softmax_pv(bufs, j, masked):
        buf_ref, bmax_ref = bufs
        start = pl.multiple_of(j * blk, blk)
        vb = v_ref[pl.ds(start, blk), :]
        if masked:
            row = lax.broadcasted_iota(jnp.int32, (blk, blk), 0)
            col = lax.broadcasted_iota(jnp.int32, (blk, blk), 1)
            keep = col <= row
        for c in range(2):
            s = buf_ref[c]
            m_old = m_ref[c]
            if masked:
                s = jnp.where(keep, s, NEG_INF)
                m_new = jnp.maximum(m_old, jnp.max(s, axis=-1, keepdims=True))
            else:
                m_new = jnp.maximum(m_old, bmax_ref[c])
            alpha = jnp.exp2(m_old - m_new)
            p = jnp.exp2(s - jnp.concatenate([m_new] * (blk // lanes), axis=1))
            part = p[:, 0:lanes]
            for t in range(1, blk // lanes):
                part = part + p[:, t * lanes:(t + 1) * lanes]
            l_ref[c] = alpha * l_ref[c] + part
            acc_ref[c] = (jnp.concatenate([alpha] * (DV_DIFF // lanes), axis=1) * acc_ref[c]
                          + jnp.dot(p.astype(BF16), vb, preferred_element_type=F32))
            m_ref[c] = m_new

    buf_a = (sa_ref, ma_ref)
    buf_b = (sb_ref, mb_ref)
    scores_into(buf_a, 0)

    def pair(t, carry):
        j = 2 * t
        scores_into(buf_b, j + 1)
        softmax_pv(buf_a, j, False)
        scores_into(buf_a, j + 2)
        softmax_pv(buf_b, j + 1, False)
        return carry

    lax.fori_loop(0, i // 2, pair, 0)

    @pl.when(i % 2 == 1)
    def _odd_tail():
        scores_into(buf_b, i)
        softmax_pv(buf_a, i - 1, False)
        softmax_pv(buf_b, i, True)

    @pl.when(i % 2 == 0)
    def _even_tail():
        softmax_pv(buf_a, i, True)

    lam = _diff_lambda_in_kernel(lq1_ref, lk1_ref, lq2_ref, lk2_ref, lam_init)
    l0 = jnp.sum(l_ref[0], axis=-1, keepdims=True)
    l1 = jnp.sum(l_ref[1], axis=-1, keepdims=True)
    o = acc_ref[0] / l0 - lam * (acc_ref[1] / l1)
    o = o * lax.rsqrt(jnp.mean(o * o, axis=-1, keepdims=True) + GN_EPS)
    o = o * gain_ref[...] * (1.0 - lam_init)
    y_ref[...] = (o * _silu(z_ref[...].astype(F32))).astype(y_ref.dtype)


def _diff_prompt(h, hkv, lam_params, gain, lam_init, batch, seq):
    blk = Q_BLOCK
    assert seq % blk == 0
    nq = seq // blk
    heads_per_seg = SEG // DV_DIFF
    small = pl.BlockSpec((1, DK_DIFF), lambda b, hh, i: (0, 0))
    return pl.pallas_call(
        functools.partial(_diff_prompt_kernel, lam_init=lam_init),
        grid=(batch, H_DIFF, nq),
        in_specs=[
            small, small, small, small,
            pl.BlockSpec((1, DV_DIFF), lambda b, hh, i: (0, 0)),
            pl.BlockSpec((blk, 2 * DK_DIFF), lambda b, hh, i: (b * nq + i, SEG_QA * heads_per_seg + hh)),
            pl.BlockSpec((seq, 2 * DK_DIFF), lambda b, hh, i: (b, hh)),
            pl.BlockSpec((seq, DV_DIFF), lambda b, hh, i: (b, heads_per_seg + hh)),
            pl.BlockSpec((blk, DV_DIFF), lambda b, hh, i: (b * nq + i, SEG_ZA * heads_per_seg + hh)),
        ],
        out_specs=pl.BlockSpec((blk, DV_DIFF), lambda b, hh, i: (b * nq + i, hh)),
        out_shape=jax.ShapeDtypeStruct((batch * seq, W_DIFF), BF16),
        scratch_shapes=[
            pltpu.VMEM((2, blk, blk), F32),
            pltpu.VMEM((2, blk, blk), F32),
            pltpu.VMEM((2, blk, LANES), F32),
            pltpu.VMEM((2, blk, LANES), F32),
            pltpu.VMEM((2, blk, LANES), F32),
            pltpu.VMEM((2, blk, LANES), F32),
            pltpu.VMEM((2, blk, DV_DIFF), F32),
        ],
        compiler_params=_params("arbitrary", "arbitrary", "arbitrary"),
        name="diff_prompt",
    )(*lam_params, gain, h, hkv, hkv, h)


def _retention_sample_kernel(lg_ref, q_ref, k_ref, kt_ref, v_ref, vall_ref, z_ref, s_in_ref,
                             y_ref, s_out_ref):
    g = pl.program_id(0)
    hh = pl.program_id(1)
    group = q_ref.shape[0]
    n_seq = kt_ref.shape[1]
    gamma = jnp.exp(jnp.full((1, DV_RET), lg_ref[hh], F32))
    q_bf = q_ref[...].astype(BF16)
    k_bf = k_ref[...].astype(BF16)
    v_bf = v_ref[...].astype(BF16)
    qk = jnp.sum(q_bf.astype(F32) * k_bf.astype(F32), axis=-1, keepdims=True)
    kt_bf = kt_ref[...].astype(BF16)
    vall_bf = vall_ref[...].astype(BF16)
    seq_lane = lax.broadcasted_iota(jnp.int32, (DK_RET, n_seq), 1)
    row = lax.broadcasted_iota(jnp.int32, (group, DV_RET), 0)
    cross = jnp.zeros((group, DV_RET), F32)
    for s in range(group):
        s_old = s_in_ref[s]
        qs = jnp.dot(q_bf, s_old.astype(BF16), preferred_element_type=F32)
        cross = jnp.where(row == s, qs, cross)
        k_only = jnp.where(seq_lane == g * group + s, kt_bf, jnp.zeros_like(kt_bf))
        outer = jnp.dot(k_only, vall_bf, preferred_element_type=F32)
        s_out_ref[s] = s_old * gamma + outer
    o = qk * v_bf.astype(F32) + cross * gamma
    mu = jnp.mean(o, axis=-1, keepdims=True)
    cen = o - mu
    var = jnp.mean(cen * cen, axis=-1, keepdims=True)
    y_ref[...] = (cen * lax.rsqrt(var + GN_EPS) * _silu(z_ref[...].astype(F32))).astype(y_ref.dtype)


def _retention_sample(hs, kt, state, log_gamma):
    n_seq = hs.shape[0]
    group = SAMPLE_GROUP
    assert n_seq % group == 0
    per_seg = SEG // DK_RET
    col = lambda seg: pl.BlockSpec((group, DK_RET), lambda g, hh, seg=seg: (g, seg * per_seg + hh))
    st = pl.BlockSpec((group, None, DK_RET, DV_RET), lambda g, hh: (g, hh, 0, 0))
    return pl.pallas_call(
        _retention_sample_kernel,
        grid=(n_seq // group, H_RET),
        in_specs=[
            pl.BlockSpec(memory_space=pltpu.SMEM),
            col(SEG_QR), col(SEG_KR),
            pl.BlockSpec((DK_RET, n_seq), lambda g, hh: (hh, 0)),
            col(SEG_VR),
            pl.BlockSpec((n_seq, DV_RET), lambda g, hh: (0, SEG_VR * per_seg + hh)),
            col(SEG_ZR),
            st,
        ],
        out_specs=[pl.BlockSpec((group, DV_RET), lambda g, hh: (g, hh)), st],
        out_shape=[
            jax.ShapeDtypeStruct((n_seq, W_RET), F32),
            jax.ShapeDtypeStruct(state.shape, F32),
        ],
        compiler_params=_params("arbitrary", "arbitrary"),
        name="retention_sample",
    )(log_gamma, hs, hs, kt, hs, hs, hs, state)


def _diff_sample_kernel(pt_ref, lq1_ref, lk1_ref, lq2_ref, lk2_ref, gain_ref, q_ref, kn_ref, vn_ref, z_ref,
                        ck_hbm, cv_hbm, y_ref, kbuf_ref, vbuf_ref, s_ref, bias_ref, ksem_ref, vsem_ref,
                        *, n_seq, layer, lam_init):
    b = pl.program_id(0)
    n_slots, n_pages, page = kbuf_ref.shape[0], kbuf_ref.shape[1], kbuf_ref.shape[2]
    rows = page * H_DIFF
    lanes = bias_ref.shape[-1]

    def page_copies(seq, slot):
        out = []
        for p in range(n_pages):
            src = pt_ref[seq, p]
            out.append(pltpu.make_async_copy(ck_hbm.at[layer, src], kbuf_ref.at[slot, p], ksem_ref.at[slot]))
            out.append(pltpu.make_async_copy(cv_hbm.at[layer, src], vbuf_ref.at[slot, p], vsem_ref.at[slot]))
        return out

    def request(seq):
        for cp in page_copies(seq, seq % n_slots):
            cp.start()

    @pl.when(b == 0)
    def _first_requests():
        for seq in range(min(n_slots - 1, n_seq)):
            request(seq)

    @pl.when(b + (n_slots - 1) < n_seq)
    def _request_ahead():
        request(b + (n_slots - 1))

    slot = b % n_slots
    for cp in page_copies(b, slot):
        cp.wait()
    k_pages = [kbuf_ref.at[slot, p] for p in range(n_pages)]
    v_pages = [vbuf_ref.at[slot, p] for p in range(n_pages)]

    @pl.when(b == 0)
    def _init_bias():
        r = lax.broadcasted_iota(jnp.int32, (rows, lanes), 0)
        j = lax.broadcasted_iota(jnp.int32, (rows, lanes), 1)
        bias_ref[...] = jnp.where((j < 2 * H_DIFF) & (r % H_DIFF == j % H_DIFF), 0.0, NEG_INF)

    q4 = q_ref[...]
    in_map0 = lax.broadcasted_iota(jnp.int32, q4.shape, 1) < DK_DIFF
    q_cols = jnp.concatenate(
        [jnp.where(in_map0, q4, 0.0), jnp.where(in_map0, 0.0, q4),
         jnp.zeros((lanes - 2 * H_DIFF, 2 * DK_DIFF), F32)], axis=0).astype(BF16)

    kn2 = jnp.concatenate([kn_ref[...], kn_ref[...]], axis=0)
    new_rows = kn2.shape[0]
    new_bias = jnp.where(lax.broadcasted_iota(jnp.int32, (new_rows, lanes), 0) < H_DIFF,
                         bias_ref[0:new_rows, :], NEG_INF)
    s_new = _nt_dot(kn2.astype(BF16), q_cols) + new_bias
    m = jnp.max(s_new, axis=0, keepdims=True)
    for p in range(n_pages):
        k_mat = k_pages[p][...].reshape(rows, 2 * DK_DIFF).astype(BF16)
        s = _nt_dot(k_mat, q_cols) + bias_ref[...]
        s_ref[p] = s
        m = jnp.maximum(m, jnp.max(s, axis=0, keepdims=True))
    e_new = jnp.exp(s_new - m)
    l = jnp.sum(e_new, axis=0, keepdims=True)
    for p in range(n_pages):
        e = jnp.exp(s_ref[p] - m)
        s_ref[p] = e
        l = l + jnp.sum(e, axis=0, keepdims=True)
    lam = _diff_lambda_in_kernel(lq1_ref, lk1_ref, lq2_ref, lk2_ref, lam_init)
    lane = lax.broadcasted_iota(jnp.int32, (1, lanes), 1)
    coef = jnp.where(lane < H_DIFF, 1.0, -lam) / l
    coef = jnp.where(lane < 2 * H_DIFF, coef, 0.0)
    spread = jnp.ones((lanes, DV_DIFF), BF16)
    a_new = jnp.dot((e_new * coef).astype(BF16), spread, preferred_element_type=F32)
    acc = a_new * jnp.concatenate([vn_ref[...], vn_ref[...]], axis=0)
    for p in range(n_pages):
        a = jnp.dot((s_ref[p] * coef).astype(BF16), spread, preferred_element_type=F32)
        av = a * v_pages[p][...].reshape(rows, DV_DIFF)
        acc = acc + jnp.sum(av.reshape(rows // new_rows, new_rows, DV_DIFF), axis=0)
    o = acc[0:H_DIFF, :] + acc[H_DIFF:new_rows, :]
    o = o * lax.rsqrt(jnp.mean(o * o, axis=-1, keepdims=True) + GN_EPS)
    o = o * gain_ref[...] * (1.0 - lam_init)
    y_ref[...] = (o * _silu(z_ref[...])).astype(y_ref.dtype)


def _diff_sample(q4, kn4, vn4, z4, cache_k, cache_v, layer, page_table, lam_params, gain, lam_init):
    n_seq = q4.shape[0]
    n_pages = page_table.shape[1]
    page = cache_k.shape[2]
    lanes = 128
    small = pl.BlockSpec((1, DK_DIFF), lambda b, pt: (0, 0))
    tok = pl.BlockSpec((None, H_DIFF, DV_DIFF), lambda b, pt: (b, 0, 0))
    in_hbm = pl.BlockSpec(memory_space=pl.ANY)
    page_slots = pltpu.VMEM((PAGE_SLOTS, n_pages, page, H_DIFF, DV_DIFF), F32)
    grid_spec = pltpu.PrefetchScalarGridSpec(
        num_scalar_prefetch=1,
        grid=(n_seq,),
        in_specs=[small, small, small, small,
                  pl.BlockSpec((1, DV_DIFF), lambda b, pt: (0, 0)),
                  tok, tok, tok, tok, in_hbm, in_hbm],
        out_specs=tok,
        scratch_shapes=[page_slots, page_slots,
                        pltpu.VMEM((n_pages, page * H_DIFF, lanes), F32),
                        pltpu.VMEM((page * H_DIFF, lanes), F32),
                        pltpu.SemaphoreType.DMA((PAGE_SLOTS,)),
                        pltpu.SemaphoreType.DMA((PAGE_SLOTS,))],
    )
    return pl.pallas_call(
        functools.partial(_diff_sample_kernel, n_seq=n_seq, layer=layer, lam_init=lam_init),
        grid_spec=grid_spec,
        out_shape=jax.ShapeDtypeStruct((n_seq, H_DIFF, DV_DIFF), F32),
        compiler_params=pltpu.CompilerParams(dimension_semantics=("arbitrary",),
                                             vmem_limit_bytes=DECODE_VMEM_LIMIT_BYTES),
        name="diff_sample",
    )(page_table, *lam_params, gain, q4, kn4, vn4, z4, cache_k, cache_v)


def _merge_rows(x_ref, yr_ref, ya_ref, gr_ref, ga_ref, wr_ref, wa_ref, wo_ref, lng_ref, lnb_ref, y_ref, alpha):
    rows = x_ref.shape[0]
    part = max(rows // MERGE_PARTS, min(rows, MERGE_MIN_PART))
    for t in range(rows // part):
        rs = slice(t * part, (t + 1) * part)
        br = jnp.dot(yr_ref[rs, :].astype(BF16), wr_ref[...], preferred_element_type=F32)
        ba = jnp.dot(ya_ref[rs, :].astype(BF16), wa_ref[...], preferred_element_type=F32)
        m = _sigmoid(gr_ref[rs, :].astype(F32)) * br + _sigmoid(ga_ref[rs, :].astype(F32)) * ba
        h = alpha * x_ref[rs, :] + jnp.dot(m.astype(BF16), wo_ref[...], preferred_element_type=F32)
        mu = jnp.mean(h, axis=-1, keepdims=True)
        cen = h - mu
        var = jnp.mean(cen * cen, axis=-1, keepdims=True)
        y_ref[rs, :] = cen * lax.rsqrt(var + LN_EPS) * lng_ref[...] + lnb_ref[...]


def _merge_kernel(x_ref, yr_ref, ya_ref, gr_ref, ga_ref, xs_ref, yrs_ref, yas_ref, grs_ref, gas_ref,
                  wr_ref, wa_ref, wo_ref, lng_ref, lnb_ref, y_ref, ys_ref, *, alpha):
    weights = (wr_ref, wa_ref, wo_ref, lng_ref, lnb_ref)
    _merge_rows(x_ref, yr_ref, ya_ref, gr_ref, ga_ref, *weights, y_ref, alpha)

    @pl.when(pl.program_id(0) == 0)
    def _sample_rows():
        _merge_rows(xs_ref, yrs_ref, yas_ref, grs_ref, gas_ref, *weights, ys_ref, alpha)


def _merge(x2d, y_r, y_a, h, xs2d, y_rs, y_as, hs, w_r, w_a, w_o, ln_gain, ln_bias, rows, alpha):
    m, d = x2d.shape
    n_s = xs2d.shape[0]
    assert m % rows == 0
    gate_blocks = (2 * SEG) // d if d <= 2 * SEG else None
    assert gate_blocks == 1, "gates are two segments wide, same as the model width"
    const = lambda shape, col=0: pl.BlockSpec(shape, lambda i: (0, col), pipeline_mode=pl.Buffered(1))
    return pl.pallas_call(
        functools.partial(_merge_kernel, alpha=alpha),
        grid=(m // rows,),
        in_specs=[
            pl.BlockSpec((rows, d), lambda i: (i, 0)),
            pl.BlockSpec((rows, W_RET), lambda i: (i, 0)),
            pl.BlockSpec((rows, W_DIFF), lambda i: (i, 0)),
            pl.BlockSpec((rows, d), lambda i: (i, SEG_GR * SEG // d)),
            pl.BlockSpec((rows, d), lambda i: (i, SEG_GA * SEG // d)),
            const((n_s, d)), const((n_s, W_RET)), const((n_s, W_DIFF)),
            const((n_s, d), SEG_GR * SEG // d), const((n_s, d), SEG_GA * SEG // d),
            const((W_RET, d)), const((W_DIFF, d)), const((d, d)),
            const((1, d)), const((1, d)),
        ],
        out_specs=[pl.BlockSpec((rows, d), lambda i: (i, 0)),
                   pl.BlockSpec((n_s, d), lambda i: (0, 0))],
        out_shape=[jax.ShapeDtypeStruct((m, d), F32), jax.ShapeDtypeStruct((n_s, d), F32)],
        compiler_params=_params("arbitrary"),
        name="merge",
    )(x2d, y_r, y_a, h, h, xs2d, y_rs, y_as, hs, hs, w_r, w_a, w_o, ln_gain, ln_bias)


def _rope_tables(pos):
    half = DK_RET // 2
    inv_freq = 1.0 / (ROPE_BASE ** (np.arange(half, dtype=np.float64) / half))
    ang = np.asarray(pos, np.float64)[:, None] * inv_freq[None, :]
    return jnp.asarray(np.cos(ang), F32), jnp.asarray(np.sin(ang), F32)


def kernel(x_prompt, x_sample, state_ret, cache_k, cache_v, page_table, w_in, w_branch_ret, w_branch_diff, w_out, lambda_q1, lambda_k1, lambda_q2, lambda_k2, subln_gain, ln_gain, ln_bias):
    batch, seq, d_model = x_prompt.shape
    n_seq, t_s, _ = x_sample.shape
    depth = w_in.shape[0]
    assert t_s == 1, "the sample kernels implement the one-new-token step"
    n_pool, page = cache_k.shape[1], cache_k.shape[2]
    past_len = page_table.shape[1] * page
    alpha = (2.0 * depth) ** 0.25

    log_gamma = jnp.asarray(np.log(1.0 - np.exp2(-5.0 - np.arange(H_RET, dtype=np.float64))), F32)
    cos_p, sin_p = _rope_tables(np.arange(seq))
    cos_s, sin_s = _rope_tables(np.full((n_seq,), past_len))

    hp = x_prompt.reshape(batch * seq, d_model)
    hs = x_sample.reshape(n_seq, d_model)
    outs = [[] for _ in range(6)]
    for l in range(depth):
        lam_init = 0.8 - 0.6 * math.exp(-0.3 * l)
        lam_params = [p[l].reshape(1, DK_DIFF) for p in (lambda_q1, lambda_k1, lambda_q2, lambda_k2)]
        gain = subln_gain[l].reshape(1, DV_DIFF)
        lng = ln_gain[l].reshape(1, d_model)
        lnb = ln_bias[l].reshape(1, d_model)

        hkv_p, k_p, v_p, hp_bf, k_s, v_s, w_r_bf, w_a_bf, w_o_bf = _kv_proj(
            hp, hs, w_in[l], [w_branch_ret[l], w_branch_diff[l], w_out[l]], KV_PROJ_ROWS)
        h_p, h_s = _in_proj(hp_bf, hs, w_in[l], cos_p, sin_p, cos_s, sin_s, PROJ_ROWS,
                            DK_DIFF ** -0.5 * LOG2E, DK_DIFF ** -0.5)

        y_r, s_p = _retention_prompt(h_p, log_gamma, batch, seq)
        y_a = _diff_prompt(h_p, hkv_p, lam_params, gain, lam_init, batch, seq)

        kt = h_s[:, SEG_KR * SEG:(SEG_KR + 1) * SEG].T
        y_rs, s_s = _retention_sample(h_s, kt, state_ret[l], log_gamma)
        tok = lambda seg: h_s[:, seg * SEG:(seg + 1) * SEG].reshape(n_seq, H_DIFF, DV_DIFF)
        y_as = _diff_sample(tok(SEG_QA), k_s, v_s, tok(SEG_ZA), cache_k, cache_v, l,
                            page_table, lam_params, gain, lam_init)

        hp, hs = _merge(hp, y_r, y_a, h_p, hs, y_rs, y_as.reshape(n_seq, W_DIFF), h_s,
                        w_r_bf, w_a_bf, w_o_bf, lng, lnb, MERGE_ROWS, alpha)

        outs[0].append(s_p)
        outs[1].append(k_p.reshape(batch, seq, H_DIFF, 2 * DK_DIFF))
        outs[2].append(v_p.reshape(batch, seq, H_DIFF, DV_DIFF))
        outs[3].append(s_s)
        outs[4].append(k_s.reshape(n_seq, t_s, H_DIFF, 2 * DK_DIFF))
        outs[5].append(v_s.reshape(n_seq, t_s, H_DIFF, DV_DIFF))

    return (hp.reshape(batch, seq, d_model), hs.reshape(n_seq, t_s, d_model),
            *[jnp.stack(o) for o in outs])
```

```python
import functools
import math

import jax
import jax.numpy as jnp
import numpy as np
from jax import lax
from jax.experimental import pallas as pl
from jax.experimental.pallas import tpu as pltpu

F32 = jnp.float32
BF16 = jnp.bfloat16

H_RET = 4
DK_RET = 256
DV_RET = 256
H_DIFF = 4
DK_DIFF = 128
DV_DIFF = 256
ROPE_BASE = 10000.0
LN_EPS = 1e-5
GN_EPS = 1e-6
NEG_INF = -1e30

SEG = 1024
W_RET = H_RET * DV_RET
W_DIFF = H_DIFF * DV_DIFF
W_SEG_KA = 5
SEG_QR, SEG_KR, SEG_VR, SEG_ZR, SEG_QA, SEG_ZA = range(6)
SEG_GR = 6
SEG_GA = 8
N_SEG = 10

LANES = 128
LOG2E = math.log2(math.e)
VMEM_LIMIT_BYTES = 56 * 1024 * 1024

RET_CHUNK = 256
Q_BLOCK = 1024
KV_PROJ_ROWS = 256
PROJ_ROWS = 1024
MERGE_ROWS = 512
MERGE_PARTS = 2
MERGE_MIN_PART = 256
SAMPLE_GROUP = 32
PAGE_SLOTS = 3
DECODE_VMEM_LIMIT_BYTES = 60 * 1024 * 1024


def _params(*sem):
    return pltpu.CompilerParams(dimension_semantics=sem, vmem_limit_bytes=VMEM_LIMIT_BYTES)


def _silu(z):
    return z * (1.0 / (1.0 + jnp.exp(-z)))


def _sigmoid(z):
    return 1.0 / (1.0 + jnp.exp(-z))


def _nt_dot(a, b):
    return lax.dot_general(a, b, (((1,), (1,)), ((), ())), preferred_element_type=F32)


def _side_cast_plan(side_weights, n_steps):
    side_rows = [sw.shape[0] for sw in side_weights]
    chunk = sum(side_rows) // n_steps
    assert chunk * n_steps == sum(side_rows) and chunk % 16 == 0 and all(r % chunk == 0 for r in side_rows)
    side_steps, first = [], 0
    for r in side_rows:
        side_steps.append((first, r // chunk))
        first += r // chunk
    return chunk, tuple(side_steps)


def _side_cast_step(step, side_in, side_out, side_steps):
    for t, (first, count) in enumerate(side_steps):
        @pl.when((step >= first) & (step < first + count))
        def _cast_side(t=t):
            side_out[t][...] = side_in[t][...].astype(BF16)


def _kv_proj_kernel(x_ref, xs_ref, wk_ref, wv_ref, hkv_ref, k_ref, v_ref, xb_ref, ks_ref, vs_ref, wkb_ref, wvb_ref):
    i = pl.program_id(0)

    @pl.when(i == 0)
    def _cast_weights():
        wkb_ref[...] = wk_ref[...].astype(BF16)
        wvb_ref[...] = wv_ref[...].astype(BF16)

    xb = x_ref[...].astype(BF16)
    xb_ref[...] = xb
    acc_k = jnp.dot(xb, wkb_ref[...], preferred_element_type=F32)
    hkv_ref[:, :SEG] = acc_k.astype(hkv_ref.dtype)
    k_ref[...] = acc_k.reshape(k_ref.shape)
    acc_v = jnp.dot(xb, wvb_ref[...], preferred_element_type=F32)
    hkv_ref[:, SEG:] = acc_v.astype(hkv_ref.dtype)
    v_ref[...] = acc_v.reshape(v_ref.shape)

    @pl.when(i == 0)
    def _sample_rows():
        xs = xs_ref[...].astype(BF16)
        ks_ref[...] = jnp.dot(xs, wkb_ref[...], preferred_element_type=F32).reshape(ks_ref.shape)
        vs_ref[...] = jnp.dot(xs, wvb_ref[...], preferred_element_type=F32).reshape(vs_ref.shape)


def _kv_proj(x2d, xs2d, w, rows):
    m, d = x2d.shape
    n_s = xs2d.shape[0]
    assert m % rows == 0
    resident = lambda shape, idx: pl.BlockSpec(shape, lambda i: idx, pipeline_mode=pl.Buffered(1))
    return pl.pallas_call(
        _kv_proj_kernel,
        grid=(m // rows,),
        in_specs=[
            pl.BlockSpec((rows, d), lambda i: (i, 0)),
            resident((n_s, d), (0, 0)),
            resident((d, SEG), (0, W_SEG_KA)),
            resident((d, SEG), (0, W_SEG_KA + 1)),
        ],
        out_specs=[
            pl.BlockSpec((rows, 2 * SEG), lambda i: (i, 0)),
            pl.BlockSpec((rows, H_DIFF, 2 * DK_DIFF), lambda i: (i, 0, 0)),
            pl.BlockSpec((rows, H_DIFF, DV_DIFF), lambda i: (i, 0, 0)),
            pl.BlockSpec((rows, d), lambda i: (i, 0)),
            pl.BlockSpec((n_s, H_DIFF, 2 * DK_DIFF), lambda i: (0, 0, 0)),
            pl.BlockSpec((n_s, H_DIFF, DV_DIFF), lambda i: (0, 0, 0)),
        ],
        out_shape=[
            jax.ShapeDtypeStruct((m, 2 * SEG), BF16),
            jax.ShapeDtypeStruct((m, H_DIFF, 2 * DK_DIFF), F32),
            jax.ShapeDtypeStruct((m, H_DIFF, DV_DIFF), F32),
            jax.ShapeDtypeStruct((m, d), BF16),
            jax.ShapeDtypeStruct((n_s, H_DIFF, 2 * DK_DIFF), F32),
            jax.ShapeDtypeStruct((n_s, H_DIFF, DV_DIFF), F32),
        ],
        scratch_shapes=[pltpu.VMEM((d, SEG), BF16), pltpu.VMEM((d, SEG), BF16)],
        compiler_params=_params("arbitrary"),
        name="kv_proj",
    )(x2d, xs2d, w, w)


def _store_segment(acc, j, cos_ref, sin_ref, h_ref, q_scale, rope):
    if rope:
        scale = jnp.where(j == SEG_KR, DK_RET ** -0.5, 1.0).astype(F32)
        cos = cos_ref[...] * scale
        sin = sin_ref[...] * scale
        half = DK_RET // 2
        for hh in range(H_RET):
            a = hh * DK_RET
            x1 = acc[:, a:a + half]
            x2 = acc[:, a + half:a + DK_RET]
            h_ref[:, a:a + half] = (x1 * cos - x2 * sin).astype(h_ref.dtype)
            h_ref[:, a + half:a + DK_RET] = (x2 * cos + x1 * sin).astype(h_ref.dtype)
    else:
        scale = jnp.where(j == SEG_QA, q_scale, 1.0).astype(F32)
        h_ref[...] = (acc * scale).astype(h_ref.dtype)


def _in_proj_kernel(xb_ref, xs_ref, w_ref, cos_ref, sin_ref, cos_s_ref, sin_s_ref, *rest,
                    q_scale, q_scale_s, side_steps):
    n_side = len(side_steps)
    side_in = rest[:n_side]
    h_ref, hs_ref = rest[n_side:n_side + 2]
    side_out = rest[n_side + 2:2 * n_side + 2]
    wb_ref, = rest[2 * n_side + 2:]
    j = pl.program_id(0)
    i = pl.program_id(1)
    _side_cast_step((j - (SEG_KR + 1)) * pl.num_programs(1) + i, side_in, side_out, side_steps)

    @pl.when(i == 0)
    def _cast_weights():
        wb_ref[...] = w_ref[...].astype(BF16)

    def project(rope):
        acc = jnp.dot(xb_ref[...], wb_ref[...], preferred_element_type=F32)
        _store_segment(acc, j, cos_ref, sin_ref, h_ref, q_scale, rope)

        @pl.when(i == 0)
        def _sample_rows():
            acc_s = jnp.dot(xs_ref[...].astype(BF16), wb_ref[...], preferred_element_type=F32)
            _store_segment(acc_s, j, cos_s_ref, sin_s_ref, hs_ref, q_scale_s, rope)

    pl.when(j <= SEG_KR)(functools.partial(project, True))
    pl.when(j > SEG_KR)(functools.partial(project, False))


def _in_proj(xb2d, xs2d, w, cos, sin, cos_s, sin_s, side_weights, rows, q_scale, q_scale_s):
    m, d = xb2d.shape
    n_s = xs2d.shape[0]
    assert w.shape[1] == (N_SEG + 2) * SEG and m % rows == 0 and cos.shape[0] % rows == 0
    n_i = m // rows
    pos_blocks = cos.shape[0] // rows
    half = DK_RET // 2
    const2 = lambda j, i: (0, 0)
    w_seg = lambda j, i: (0, jnp.where(j < W_SEG_KA, j, j + 2))
    chunk, side_steps = _side_cast_plan(side_weights, (N_SEG - (SEG_KR + 1)) * n_i)

    def side_spec(t):
        first, count = side_steps[t]
        return pl.BlockSpec((chunk, side_weights[t].shape[1]),
                            lambda j, i: (jnp.clip((j - (SEG_KR + 1)) * n_i + i - first, 0, count - 1), 0))

    side_specs = [side_spec(t) for t in range(len(side_weights))]
    return pl.pallas_call(
        functools.partial(_in_proj_kernel, q_scale=q_scale, q_scale_s=q_scale_s, side_steps=side_steps),
        grid=(N_SEG, n_i),
        in_specs=[
            pl.BlockSpec((rows, d), lambda j, i: (i, 0)),
            pl.BlockSpec((n_s, d), const2),
            pl.BlockSpec((d, SEG), w_seg),
            pl.BlockSpec((rows, half), lambda j, i: (i % pos_blocks, 0)),
            pl.BlockSpec((rows, half), lambda j, i: (i % pos_blocks, 0)),
            pl.BlockSpec((n_s, half), const2),
            pl.BlockSpec((n_s, half), const2),
        ] + side_specs,
        out_specs=[
            pl.BlockSpec((rows, SEG), lambda j, i: (i, j)),
            pl.BlockSpec((n_s, SEG), lambda j, i: (0, j)),
        ] + side_specs,
        out_shape=[
            jax.ShapeDtypeStruct((m, N_SEG * SEG), BF16),
            jax.ShapeDtypeStruct((n_s, N_SEG * SEG), F32),
        ] + [jax.ShapeDtypeStruct(sw.shape, BF16) for sw in side_weights],
        scratch_shapes=[pltpu.VMEM((d, SEG), BF16)],
        compiler_params=_params("arbitrary", "arbitrary"),
        name="in_proj",
    )(xb2d, xs2d, w, cos, sin, cos_s, sin_s, *side_weights)


def _retention_prompt_kernel(lg_ref, q_ref, k_ref, v_ref, z_ref, y_ref, s_out_ref,
                             s_ref, decay_ref, cross_ref, kdec_ref):
    c = pl.program_id(1)
    chunk = q_ref.shape[0]

    @pl.when(c == 0)
    def _init():
        s_ref[...] = jnp.zeros_like(s_ref)
        row = lax.broadcasted_iota(jnp.int32, (chunk, chunk), 0)
        col = lax.broadcasted_iota(jnp.int32, (chunk, chunk), 1)
        diff = (row - col).astype(F32)
        pos = lax.broadcasted_iota(jnp.int32, (chunk, DV_RET), 0).astype(F32)
        for hh in range(H_RET):
            lg = lg_ref[hh]
            decay_ref[hh] = jnp.where(diff >= 0, jnp.exp(lg * jnp.maximum(diff, 0.0)), 0.0)
            cross_ref[hh] = jnp.exp(lg * (pos + 1.0))
            kdec_ref[hh] = jnp.exp(lg * (chunk - 1.0 - pos))

    for hh in range(H_RET):
        sl = slice(hh * DK_RET, (hh + 1) * DK_RET)
        q = q_ref[:, sl]
        k = k_ref[:, sl]
        v = v_ref[:, sl]
        s_old = s_ref[hh]
        intra = (_nt_dot(q, k) * decay_ref[hh]).astype(BF16)
        o = jnp.dot(intra, v, preferred_element_type=F32)
        o = o + jnp.dot(q, s_old.astype(BF16), preferred_element_type=F32) * cross_ref[hh]
        k_dec = (k.astype(F32) * kdec_ref[hh]).astype(BF16)
        state_decay = jnp.exp(jnp.full((1, DV_RET), lg_ref[hh] * chunk, F32))
        s_new = s_old * state_decay + lax.dot_general(
            k_dec, v, (((0,), (0,)), ((), ())), preferred_element_type=F32)
        s_ref[hh] = s_new
        mu = jnp.mean(o, axis=-1, keepdims=True)
        cen = o - mu
        var = jnp.mean(cen * cen, axis=-1, keepdims=True)
        gate = _silu(z_ref[:, sl].astype(F32))
        y_ref[:, sl] = (cen * lax.rsqrt(var + GN_EPS) * gate).astype(y_ref.dtype)

    @pl.when(c == pl.num_programs(1) - 1)
    def _emit_state():
        s_out_ref[...] = s_ref[...]


def _retention_prompt(h, log_gamma, batch, seq):
    chunk = RET_CHUNK
    assert seq % chunk == 0
    nc = seq // chunk
    blk = lambda seg: pl.BlockSpec((chunk, SEG), lambda b, c, seg=seg: (b * nc + c, seg))
    return pl.pallas_call(
        _retention_prompt_kernel,
        grid=(batch, nc),
        in_specs=[pl.BlockSpec(memory_space=pltpu.SMEM),
                  blk(SEG_QR), blk(SEG_KR), blk(SEG_VR), blk(SEG_ZR)],
        out_specs=[
            pl.BlockSpec((chunk, W_RET), lambda b, c: (b * nc + c, 0)),
            pl.BlockSpec((None, H_RET, DK_RET, DV_RET), lambda b, c: (b, 0, 0, 0)),
        ],
        out_shape=[
            jax.ShapeDtypeStruct((batch * seq, W_RET), BF16),
            jax.ShapeDtypeStruct((batch, H_RET, DK_RET, DV_RET), F32),
        ],
        scratch_shapes=[
            pltpu.VMEM((H_RET, DK_RET, DV_RET), F32),
            pltpu.VMEM((H_RET, chunk, chunk), F32),
            pltpu.VMEM((H_RET, chunk, DV_RET), F32),
            pltpu.VMEM((H_RET, chunk, DK_RET), F32),
        ],
        compiler_params=_params("arbitrary", "arbitrary"),
        name="retention_prompt",
    )(log_gamma, h, h, h, h)


def _diff_lambda_in_kernel(lq1_ref, lk1_ref, lq2_ref, lk2_ref, lam_init):
    a = jnp.sum(lq1_ref[...] * lk1_ref[...], axis=-1, keepdims=True)
    b = jnp.sum(lq2_ref[...] * lk2_ref[...], axis=-1, keepdims=True)
    return jnp.exp(a) - jnp.exp(b) + lam_init


def _diff_prompt_kernel(lq1_ref, lk1_ref, lq2_ref, lk2_ref, gain_ref, q_ref, k_ref, v_ref, z_ref, y_ref,
                        sa_ref, sb_ref, ma_ref, mb_ref, m_ref, l_ref, acc_ref, *, lam_init):
    i = pl.program_id(2)
    blk = q_ref.shape[0]
    lanes = m_ref.shape[-1]
    qs = (q_ref[:, :DK_DIFF], q_ref[:, DK_DIFF:])

    m_ref[...] = jnp.full_like(m_ref, NEG_INF)
    l_ref[...] = jnp.zeros_like(l_ref)
    acc_ref[...] = jnp.zeros_like(acc_ref)

    def scores_into(bufs, j):
        buf_ref, bmax_ref = bufs
        start = pl.multiple_of(j * blk, blk)
        kb = k_ref[pl.ds(start, blk), :]
        for c in range(2):
            s = _nt_dot(qs[c], kb[:, c * DK_DIFF:(c + 1) * DK_DIFF])
            buf_ref[c] = s
            bmax_ref[c] = jnp.broadcast_to(jnp.max(s, axis=-1, keepdims=True), (blk, lanes))

    def softmax_pv(bufs, j, masked):
        buf_ref, bmax_ref = bufs
        start = pl.multiple_of(j * blk, blk)
        vb = v_ref[pl.ds(start, blk), :]
        if masked:
            row = lax.broadcasted_iota(jnp.int32, (blk, blk), 0)
            col = lax.broadcasted_iota(jnp.int32, (blk, blk), 1)
            keep = col <= row
        for c in range(2):
            s = buf_ref[c]
            m_old = m_ref[c]
            if masked:
                s = jnp.where(keep, s, NEG_INF)
                m_new = jnp.maximum(m_old, jnp.max(s, axis=-1, keepdims=True))
            else:
                m_new = jnp.maximum(m_old, bmax_ref[c])
            alpha = jnp.exp2(m_old - m_new)
            p = jnp.exp2(s - jnp.concatenate([m_new] * (blk // lanes), axis=1))
            part = p[:, 0:lanes]
            for t in range(1, blk // lanes):
                part = part + p[:, t * lanes:(t + 1) * lanes]
            l_ref[c] = alpha * l_ref[c] + part
            acc_ref[c] = (jnp.concatenate([alpha] * (DV_DIFF // lanes), axis=1) * acc_ref[c]
                          + jnp.dot(p.astype(BF16), vb, preferred_element_type=F32))
            m_ref[c] = m_new

    buf_a = (sa_ref, ma_ref)
    buf_b = (sb_ref, mb_ref)
    scores_into(buf_a, 0)

    def pair(t, carry):
        j = 2 * t
        scores_into(buf_b, j + 1)
        softmax_pv(buf_a, j, False)
        scores_into(buf_a, j + 2)
        softmax_pv(buf_b, j + 1, False)
        return carry

    lax.fori_loop(0, i // 2, pair, 0)

    @pl.when(i % 2 == 1)
    def _odd_tail():
        scores_into(buf_b, i)
        softmax_pv(buf_a, i - 1, False)
        softmax_pv(buf_b, i, True)

    @pl.when(i % 2 == 0)
    def _even_tail():
        softmax_pv(buf_a, i, True)

    lam = _diff_lambda_in_kernel(lq1_ref, lk1_ref, lq2_ref, lk2_ref, lam_init)
    l0 = jnp.sum(l_ref[0], axis=-1, keepdims=True)
    l1 = jnp.sum(l_ref[1], axis=-1, keepdims=True)
    o = acc_ref[0] / l0 - lam * (acc_ref[1] / l1)
    o = o * lax.rsqrt(jnp.mean(o * o, axis=-1, keepdims=True) + GN_EPS)
    o = o * gain_ref[...] * (1.0 - lam_init)
    y_ref[...] = (o * _silu(z_ref[...].astype(F32))).astype(y_ref.dtype)


def _diff_prompt(h, hkv, lam_params, gain, lam_init, batch, seq):
    blk = Q_BLOCK
    assert seq % blk == 0
    nq = seq // blk
    heads_per_seg = SEG // DV_DIFF
    small = pl.BlockSpec((1, DK_DIFF), lambda b, hh, i: (0, 0))
    return pl.pallas_call(
        functools.partial(_diff_prompt_kernel, lam_init=lam_init),
        grid=(batch, H_DIFF, nq),
        in_specs=[
            small, small, small, small,
            pl.BlockSpec((1, DV_DIFF), lambda b, hh, i: (0, 0)),
            pl.BlockSpec((blk, 2 * DK_DIFF), lambda b, hh, i: (b * nq + i, SEG_QA * heads_per_seg + hh)),
            pl.BlockSpec((seq, 2 * DK_DIFF), lambda b, hh, i: (b, hh)),
            pl.BlockSpec((seq, DV_DIFF), lambda b, hh, i: (b, heads_per_seg + hh)),
            pl.BlockSpec((blk, DV_DIFF), lambda b, hh, i: (b * nq + i, SEG_ZA * heads_per_seg + hh)),
        ],
        out_specs=pl.BlockSpec((blk, DV_DIFF), lambda b, hh, i: (b * nq + i, hh)),
        out_shape=jax.ShapeDtypeStruct((batch * seq, W_DIFF), BF16),
        scratch_shapes=[
            pltpu.VMEM((2, blk, blk), F32),
            pltpu.VMEM((2, blk, blk), F32),
            pltpu.VMEM((2, blk, LANES), F32),
            pltpu.VMEM((2, blk, LANES), F32),
            pltpu.VMEM((2, blk, LANES), F32),
            pltpu.VMEM((2, blk, LANES), F32),
            pltpu.VMEM((2, blk, DV_DIFF), F32),
        ],
        compiler_params=_params("arbitrary", "arbitrary", "arbitrary"),
        name="diff_prompt",
    )(*lam_params, gain, h, hkv, hkv, h)


def _retention_sample_kernel(lg_ref, q_ref, k_ref, kt_ref, v_ref, vall_ref, z_ref, s_in_ref,
                             y_ref, s_out_ref):
    g = pl.program_id(0)
    hh = pl.program_id(1)
    group = q_ref.shape[0]
    n_seq = kt_ref.shape[1]
    gamma = jnp.exp(jnp.full((1, DV_RET), lg_ref[hh], F32))
    q_bf = q_ref[...].astype(BF16)
    k_bf = k_ref[...].astype(BF16)
    v_bf = v_ref[...].astype(BF16)
    qk = jnp.sum(q_bf.astype(F32) * k_bf.astype(F32), axis=-1, keepdims=True)
    kt_bf = kt_ref[...].astype(BF16)
    vall_bf = vall_ref[...].astype(BF16)
    seq_lane = lax.broadcasted_iota(jnp.int32, (DK_RET, n_seq), 1)
    row = lax.broadcasted_iota(jnp.int32, (group, DV_RET), 0)
    cross = jnp.zeros((group, DV_RET), F32)
    for s in range(group):
        s_old = s_in_ref[s]
        qs = jnp.dot(q_bf, s_old.astype(BF16), preferred_element_type=F32)
        cross = jnp.where(row == s, qs, cross)
        k_only = jnp.where(seq_lane == g * group + s, kt_bf, jnp.zeros_like(kt_bf))
        outer = jnp.dot(k_only, vall_bf, preferred_element_type=F32)
        s_out_ref[s] = s_old * gamma + outer
    o = qk * v_bf.astype(F32) + cross * gamma
    mu = jnp.mean(o, axis=-1, keepdims=True)
    cen = o - mu
    var = jnp.mean(cen * cen, axis=-1, keepdims=True)
    y_ref[...] = (cen * lax.rsqrt(var + GN_EPS) * _silu(z_ref[...].astype(F32))).astype(y_ref.dtype)


def _retention_sample(hs, kt, state, log_gamma):
    n_seq = hs.shape[0]
    group = SAMPLE_GROUP
    assert n_seq % group == 0
    per_seg = SEG // DK_RET
    col = lambda seg: pl.BlockSpec((group, DK_RET), lambda g, hh, seg=seg: (g, seg * per_seg + hh))
    st = pl.BlockSpec((group, None, DK_RET, DV_RET), lambda g, hh: (g, hh, 0, 0))
    return pl.pallas_call(
        _retention_sample_kernel,
        grid=(n_seq // group, H_RET),
        in_specs=[
            pl.BlockSpec(memory_space=pltpu.SMEM),
            col(SEG_QR), col(SEG_KR),
            pl.BlockSpec((DK_RET, n_seq), lambda g, hh: (hh, 0)),
            col(SEG_VR),
            pl.BlockSpec((n_seq, DV_RET), lambda g, hh: (0, SEG_VR * per_seg + hh)),
            col(SEG_ZR),
            st,
        ],
        out_specs=[pl.BlockSpec((group, DV_RET), lambda g, hh: (g, hh)), st],
        out_shape=[
            jax.ShapeDtypeStruct((n_seq, W_RET), F32),
            jax.ShapeDtypeStruct(state.shape, F32),
        ],
        compiler_params=_params("arbitrary", "arbitrary"),
        name="retention_sample",
    )(log_gamma, hs, hs, kt, hs, hs, hs, state)


def _diff_sample_kernel(pt_ref, lq1_ref, lk1_ref, lq2_ref, lk2_ref, gain_ref, q_ref, kn_ref, vn_ref, z_ref,
                        ck_hbm, cv_hbm, y_ref, kbuf_ref, vbuf_ref, s_ref, bias_ref, ksem_ref, vsem_ref,
                        *, n_seq, layer, lam_init):
    b = pl.program_id(0)
    n_slots, n_pages, page = kbuf_ref.shape[0], kbuf_ref.shape[1], kbuf_ref.shape[2]
    rows = page * H_DIFF
    lanes = bias_ref.shape[-1]

    def page_copies(seq, slot):
        out = []
        for p in range(n_pages):
            src = pt_ref[seq, p]
            out.append(pltpu.make_async_copy(ck_hbm.at[layer, src], kbuf_ref.at[slot, p], ksem_ref.at[slot]))
            out.append(pltpu.make_async_copy(cv_hbm.at[layer, src], vbuf_ref.at[slot, p], vsem_ref.at[slot]))
        return out

    def request(seq):
        for cp in page_copies(seq, seq % n_slots):
            cp.start()

    @pl.when(b == 0)
    def _first_requests():
        for seq in range(min(n_slots - 1, n_seq)):
            request(seq)

    @pl.when(b + (n_slots - 1) < n_seq)
    def _request_ahead():
        request(b + (n_slots - 1))

    slot = b % n_slots
    for cp in page_copies(b, slot):
        cp.wait()
    k_pages = [kbuf_ref.at[slot, p] for p in range(n_pages)]
    v_pages = [vbuf_ref.at[slot, p] for p in range(n_pages)]

    @pl.when(b == 0)
    def _init_bias():
        r = lax.broadcasted_iota(jnp.int32, (rows, lanes), 0)
        j = lax.broadcasted_iota(jnp.int32, (rows, lanes), 1)
        bias_ref[...] = jnp.where((j < 2 * H_DIFF) & (r % H_DIFF == j % H_DIFF), 0.0, NEG_INF)

    q4 = q_ref[...]
    in_map0 = lax.broadcasted_iota(jnp.int32, q4.shape, 1) < DK_DIFF
    q_cols = jnp.concatenate(
        [jnp.where(in_map0, q4, 0.0), jnp.where(in_map0, 0.0, q4),
         jnp.zeros((lanes - 2 * H_DIFF, 2 * DK_DIFF), F32)], axis=0).astype(BF16)

    kn2 = jnp.concatenate([kn_ref[...], kn_ref[...]], axis=0)
    new_rows = kn2.shape[0]
    new_bias = jnp.where(lax.broadcasted_iota(jnp.int32, (new_rows, lanes), 0) < H_DIFF,
                         bias_ref[0:new_rows, :], NEG_INF)
    s_new = _nt_dot(kn2.astype(BF16), q_cols) + new_bias
    m = jnp.max(s_new, axis=0, keepdims=True)
    for p in range(n_pages):
        k_mat = k_pages[p][...].reshape(rows, 2 * DK_DIFF).astype(BF16)
        s = _nt_dot(k_mat, q_cols) + bias_ref[...]
        s_ref[p] = s
        m = jnp.maximum(m, jnp.max(s, axis=0, keepdims=True))
    e_new = jnp.exp(s_new - m)
    l = jnp.sum(e_new, axis=0, keepdims=True)
    for p in range(n_pages):
        e = jnp.exp(s_ref[p] - m)
        s_ref[p] = e
        l = l + jnp.sum(e, axis=0, keepdims=True)
    lam = _diff_lambda_in_kernel(lq1_ref, lk1_ref, lq2_ref, lk2_ref, lam_init)
    lane = lax.broadcasted_iota(jnp.int32, (1, lanes), 1)
    coef = jnp.where(lane < H_DIFF, 1.0, -lam) / l
    coef = jnp.where(lane < 2 * H_DIFF, coef, 0.0)
    spread = jnp.ones((lanes, DV_DIFF), BF16)
    a_new = jnp.dot((e_new * coef).astype(BF16), spread, preferred_element_type=F32)
    acc = a_new * jnp.concatenate([vn_ref[...], vn_ref[...]], axis=0)
    for p in range(n_pages):
        a = jnp.dot((s_ref[p] * coef).astype(BF16), spread, preferred_element_type=F32)
        av = a * v_pages[p][...].reshape(rows, DV_DIFF)
        acc = acc + jnp.sum(av.reshape(rows // new_rows, new_rows, DV_DIFF), axis=0)
    o = acc[0:H_DIFF, :] + acc[H_DIFF:new_rows, :]
    o = o * lax.rsqrt(jnp.mean(o * o, axis=-1, keepdims=True) + GN_EPS)
    o = o * gain_ref[...] * (1.0 - lam_init)
    y_ref[...] = (o * _silu(z_ref[...])).astype(y_ref.dtype)


def _diff_sample(q4, kn4, vn4, z4, cache_k, cache_v, layer, page_table, lam_params, gain, lam_init):
    n_seq = q4.shape[0]
    n_pages = page_table.shape[1]
    page = cache_k.shape[2]
    lanes = 128
    small = pl.BlockSpec((1, DK_DIFF), lambda b, pt: (0, 0))
    tok = pl.BlockSpec((None, H_DIFF, DV_DIFF), lambda b, pt: (b, 0, 0))
    in_hbm = pl.BlockSpec(memory_space=pl.ANY)
    page_slots = pltpu.VMEM((PAGE_SLOTS, n_pages, page, H_DIFF, DV_DIFF), F32)
    grid_spec = pltpu.PrefetchScalarGridSpec(
        num_scalar_prefetch=1,
        grid=(n_seq,),
        in_specs=[small, small, small, small,
                  pl.BlockSpec((1, DV_DIFF), lambda b, pt: (0, 0)),
                  tok, tok, tok, tok, in_hbm, in_hbm],
        out_specs=tok,
        scratch_shapes=[page_slots, page_slots,
                        pltpu.VMEM((n_pages, page * H_DIFF, lanes), F32),
                        pltpu.VMEM((page * H_DIFF, lanes), F32),
                        pltpu.SemaphoreType.DMA((PAGE_SLOTS,)),
                        pltpu.SemaphoreType.DMA((PAGE_SLOTS,))],
    )
    return pl.pallas_call(
        functools.partial(_diff_sample_kernel, n_seq=n_seq, layer=layer, lam_init=lam_init),
        grid_spec=grid_spec,
        out_shape=jax.ShapeDtypeStruct((n_seq, H_DIFF, DV_DIFF), F32),
        compiler_params=pltpu.CompilerParams(dimension_semantics=("arbitrary",),
                                             vmem_limit_bytes=DECODE_VMEM_LIMIT_BYTES),
        name="diff_sample",
    )(page_table, *lam_params, gain, q4, kn4, vn4, z4, cache_k, cache_v)


def _merge_rows(x_ref, yr_ref, ya_ref, gr_ref, ga_ref, wr_ref, wa_ref, wo_ref, lng_ref, lnb_ref, y_ref, alpha):
    rows = x_ref.shape[0]
    part = max(rows // MERGE_PARTS, min(rows, MERGE_MIN_PART))
    for t in range(rows // part):
        rs = slice(t * part, (t + 1) * part)
        br = jnp.dot(yr_ref[rs, :].astype(BF16), wr_ref[...], preferred_element_type=F32)
        ba = jnp.dot(ya_ref[rs, :].astype(BF16), wa_ref[...], preferred_element_type=F32)
        m = _sigmoid(gr_ref[rs, :].astype(F32)) * br + _sigmoid(ga_ref[rs, :].astype(F32)) * ba
        h = alpha * x_ref[rs, :] + jnp.dot(m.astype(BF16), wo_ref[...], preferred_element_type=F32)
        mu = jnp.mean(h, axis=-1, keepdims=True)
        cen = h - mu
        var = jnp.mean(cen * cen, axis=-1, keepdims=True)
        y_ref[rs, :] = cen * lax.rsqrt(var + LN_EPS) * lng_ref[...] + lnb_ref[...]


def _merge_kernel(x_ref, yr_ref, ya_ref, gr_ref, ga_ref, xs_ref, yrs_ref, yas_ref, grs_ref, gas_ref,
                  wr_ref, wa_ref, wo_ref, lng_ref, lnb_ref, y_ref, ys_ref, *, alpha):
    weights = (wr_ref, wa_ref, wo_ref, lng_ref, lnb_ref)
    _merge_rows(x_ref, yr_ref, ya_ref, gr_ref, ga_ref, *weights, y_ref, alpha)

    @pl.when(pl.program_id(0) == 0)
    def _sample_rows():
        _merge_rows(xs_ref, yrs_ref, yas_ref, grs_ref, gas_ref, *weights, ys_ref, alpha)


def _merge(x2d, y_r, y_a, h, xs2d, y_rs, y_as, hs, w_r, w_a, w_o, ln_gain, ln_bias, rows, alpha):
    m, d = x2d.shape
    n_s = xs2d.shape[0]
    assert m % rows == 0
    gate_blocks = (2 * SEG) // d if d <= 2 * SEG else None
    assert gate_blocks == 1, "gates are two segments wide, same as the model width"
    const = lambda shape, col=0: pl.BlockSpec(shape, lambda i: (0, col), pipeline_mode=pl.Buffered(1))
    return pl.pallas_call(
        functools.partial(_merge_kernel, alpha=alpha),
        grid=(m // rows,),
        in_specs=[
            pl.BlockSpec((rows, d), lambda i: (i, 0)),
            pl.BlockSpec((rows, W_RET), lambda i: (i, 0)),
            pl.BlockSpec((rows, W_DIFF), lambda i: (i, 0)),
            pl.BlockSpec((rows, d), lambda i: (i, SEG_GR * SEG // d)),
            pl.BlockSpec((rows, d), lambda i: (i, SEG_GA * SEG // d)),
            const((n_s, d)), const((n_s, W_RET)), const((n_s, W_DIFF)),
            const((n_s, d), SEG_GR * SEG // d), const((n_s, d), SEG_GA * SEG // d),
            const((W_RET, d)), const((W_DIFF, d)), const((d, d)),
            const((1, d)), const((1, d)),
        ],
        out_specs=[pl.BlockSpec((rows, d), lambda i: (i, 0)),
                   pl.BlockSpec((n_s, d), lambda i: (0, 0))],
        out_shape=[jax.ShapeDtypeStruct((m, d), F32), jax.ShapeDtypeStruct((n_s, d), F32)],
        compiler_params=_params("arbitrary"),
        name="merge",
    )(x2d, y_r, y_a, h, h, xs2d, y_rs, y_as, hs, hs, w_r, w_a, w_o, ln_gain, ln_bias)


def _rope_tables(pos):
    half = DK_RET // 2
    inv_freq = 1.0 / (ROPE_BASE ** (np.arange(half, dtype=np.float64) / half))
    ang = np.asarray(pos, np.float64)[:, None] * inv_freq[None, :]
    return jnp.asarray(np.cos(ang), F32), jnp.asarray(np.sin(ang), F32)


def kernel(x_prompt, x_sample, state_ret, cache_k, cache_v, page_table, w_in, w_branch_ret, w_branch_diff, w_out, lambda_q1, lambda_k1, lambda_q2, lambda_k2, subln_gain, ln_gain, ln_bias):
    batch, seq, d_model = x_prompt.shape
    n_seq, t_s, _ = x_sample.shape
    depth = w_in.shape[0]
    assert t_s == 1, "the sample kernels implement the one-new-token step"
    n_pool, page = cache_k.shape[1], cache_k.shape[2]
    past_len = page_table.shape[1] * page
    alpha = (2.0 * depth) ** 0.25

    log_gamma = jnp.asarray(np.log(1.0 - np.exp2(-5.0 - np.arange(H_RET, dtype=np.float64))), F32)
    cos_p, sin_p = _rope_tables(np.arange(seq))
    cos_s, sin_s = _rope_tables(np.full((n_seq,), past_len))

    hp = x_prompt.reshape(batch * seq, d_model)
    hs = x_sample.reshape(n_seq, d_model)
    outs = [[] for _ in range(6)]
    for l in range(depth):
        lam_init = 0.8 - 0.6 * math.exp(-0.3 * l)
        lam_params = [p[l].reshape(1, DK_DIFF) for p in (lambda_q1, lambda_k1, lambda_q2, lambda_k2)]
        gain = subln_gain[l].reshape(1, DV_DIFF)
        lng = ln_gain[l].reshape(1, d_model)
        lnb = ln_bias[l].reshape(1, d_model)

        hkv_p, k_p, v_p, hp_bf, k_s, v_s = _kv_proj(hp, hs, w_in[l], KV_PROJ_ROWS)
        h_p, h_s, w_r_bf, w_a_bf, w_o_bf = _in_proj(
            hp_bf, hs, w_in[l], cos_p, sin_p, cos_s, sin_s, [w_branch_ret[l], w_branch_diff[l], w_out[l]],
            PROJ_ROWS, DK_DIFF ** -0.5 * LOG2E, DK_DIFF ** -0.5)

        y_r, s_p = _retention_prompt(h_p, log_gamma, batch, seq)
        y_a = _diff_prompt(h_p, hkv_p, lam_params, gain, lam_init, batch, seq)

        kt = h_s[:, SEG_KR * SEG:(SEG_KR + 1) * SEG].T
        y_rs, s_s = _retention_sample(h_s, kt, state_ret[l], log_gamma)
        tok = lambda seg: h_s[:, seg * SEG:(seg + 1) * SEG].reshape(n_seq, H_DIFF, DV_DIFF)
        y_as = _diff_sample(tok(SEG_QA), k_s, v_s, tok(SEG_ZA), cache_k, cache_v, l,
                            page_table, lam_params, gain, lam_init)

        hp, hs = _merge(hp, y_r, y_a, h_p, hs, y_rs, y_as.reshape(n_seq, W_DIFF), h_s,
                        w_r_bf, w_a_bf, w_o_bf, lng, lnb, MERGE_ROWS, alpha)

        outs[0].append(s_p)
        outs[1].append(k_p.reshape(batch, seq, H_DIFF, 2 * DK_DIFF))
        outs[2].append(v_p.reshape(batch, seq, H_DIFF, DV_DIFF))
        outs[3].append(s_s)
        outs[4].append(k_s.reshape(n_seq, t_s, H_DIFF, 2 * DK_DIFF))
        outs[5].append(v_s.reshape(n_seq, t_s, H_DIFF, DV_DIFF))

    return (hp.reshape(batch, seq, d_model), hs.reshape(n_seq, t_s, d_model),
            *[jnp.stack(o) for o in outs])
```

```python
import functools
import math

import jax
import jax.numpy as jnp
import numpy as np
from jax import lax
from jax.experimental import pallas as pl
from jax.experimental.pallas import tpu as pltpu

F32 = jnp.float32
BF16 = jnp.bfloat16

H_RET = 4
DK_RET = 256
DV_RET = 256
H_DIFF = 4
DK_DIFF = 128
DV_DIFF = 256
ROPE_BASE = 10000.0
LN_EPS = 1e-5
GN_EPS = 1e-6
NEG_INF = -1e30

SEG = 1024
W_RET = H_RET * DV_RET
W_DIFF = H_DIFF * DV_DIFF
W_SEG_KA = 5
SEG_QR, SEG_KR, SEG_VR, SEG_ZR, SEG_QA, SEG_ZA = range(6)
SEG_GR = 6
SEG_GA = 8
N_SEG = 10

LANES = 128
LOG2E = math.log2(math.e)
VMEM_LIMIT_BYTES = 56 * 1024 * 1024

RET_CHUNK = 256
Q_BLOCK = 1024
KV_PROJ_ROWS = 512
PROJ_ROWS = 1024
MERGE_ROWS = 512
MERGE_PARTS = 2
MERGE_MIN_PART = 256
SAMPLE_GROUP = 32
PAGE_SLOTS = 3
DECODE_VMEM_LIMIT_BYTES = 60 * 1024 * 1024


def _params(*sem):
    return pltpu.CompilerParams(dimension_semantics=sem, vmem_limit_bytes=VMEM_LIMIT_BYTES)


def _silu(z):
    return z * (1.0 / (1.0 + jnp.exp(-z)))


def _sigmoid(z):
    return 1.0 / (1.0 + jnp.exp(-z))


def _nt_dot(a, b):
    return lax.dot_general(a, b, (((1,), (1,)), ((), ())), preferred_element_type=F32)


def _side_cast_plan(side_weights, n_steps):
    side_rows = [sw.shape[0] for sw in side_weights]
    chunk = sum(side_rows) // n_steps
    assert chunk * n_steps == sum(side_rows) and chunk % 16 == 0 and all(r % chunk == 0 for r in side_rows)
    side_steps, first = [], 0
    for r in side_rows:
        side_steps.append((first, r // chunk))
        first += r // chunk
    return chunk, tuple(side_steps)


def _side_cast_step(step, side_in, side_out, side_steps):
    for t, (first, count) in enumerate(side_steps):
        @pl.when((step >= first) & (step < first + count))
        def _cast_side(t=t):
            side_out[t][...] = side_in[t][...].astype(BF16)


def _kv_proj_kernel(x_ref, xs_ref, wk_ref, wv_ref, hkv_ref, k_ref, v_ref, xb_ref, ks_ref, vs_ref, wkb_ref, wvb_ref):
    i = pl.program_id(0)

    @pl.when(i == 0)
    def _cast_weights():
        wkb_ref[...] = wk_ref[...].astype(BF16)
        wvb_ref[...] = wv_ref[...].astype(BF16)

    xb = x_ref[...].astype(BF16)
    xb_ref[...] = xb
    acc_k = jnp.dot(xb, wkb_ref[...], preferred_element_type=F32)
    hkv_ref[:, :SEG] = acc_k.astype(hkv_ref.dtype)
    k_ref[...] = acc_k.reshape(k_ref.shape)
    acc_v = jnp.dot(xb, wvb_ref[...], preferred_element_type=F32)
    hkv_ref[:, SEG:] = acc_v.astype(hkv_ref.dtype)
    v_ref[...] = acc_v.reshape(v_ref.shape)

    @pl.when(i == 0)
    def _sample_rows():
        xs = xs_ref[...].astype(BF16)
        ks_ref[...] = jnp.dot(xs, wkb_ref[...], preferred_element_type=F32).reshape(ks_ref.shape)
        vs_ref[...] = jnp.dot(xs, wvb_ref[...], preferred_element_type=F32).reshape(vs_ref.shape)


def _kv_proj(x2d, xs2d, w, rows):
    m, d = x2d.shape
    n_s = xs2d.shape[0]
    assert m % rows == 0
    resident = lambda shape, idx: pl.BlockSpec(shape, lambda i: idx, pipeline_mode=pl.Buffered(1))
    return pl.pallas_call(
        _kv_proj_kernel,
        grid=(m // rows,),
        in_specs=[
            pl.BlockSpec((rows, d), lambda i: (i, 0)),
            resident((n_s, d), (0, 0)),
            resident((d, SEG), (0, W_SEG_KA)),
            resident((d, SEG), (0, W_SEG_KA + 1)),
        ],
        out_specs=[
            pl.BlockSpec((rows, 2 * SEG), lambda i: (i, 0)),
            pl.BlockSpec((rows, H_DIFF, 2 * DK_DIFF), lambda i: (i, 0, 0)),
            pl.BlockSpec((rows, H_DIFF, DV_DIFF), lambda i: (i, 0, 0)),
            pl.BlockSpec((rows, d), lambda i: (i, 0)),
            pl.BlockSpec((n_s, H_DIFF, 2 * DK_DIFF), lambda i: (0, 0, 0)),
            pl.BlockSpec((n_s, H_DIFF, DV_DIFF), lambda i: (0, 0, 0)),
        ],
        out_shape=[
            jax.ShapeDtypeStruct((m, 2 * SEG), BF16),
            jax.ShapeDtypeStruct((m, H_DIFF, 2 * DK_DIFF), F32),
            jax.ShapeDtypeStruct((m, H_DIFF, DV_DIFF), F32),
            jax.ShapeDtypeStruct((m, d), BF16),
            jax.ShapeDtypeStruct((n_s, H_DIFF, 2 * DK_DIFF), F32),
            jax.ShapeDtypeStruct((n_s, H_DIFF, DV_DIFF), F32),
        ],
        scratch_shapes=[pltpu.VMEM((d, SEG), BF16), pltpu.VMEM((d, SEG), BF16)],
        compiler_params=_params("arbitrary"),
        name="kv_proj",
    )(x2d, xs2d, w, w)


def _store_segment(acc, j, cos_ref, sin_ref, h_ref, q_scale, rope):
    if rope:
        scale = jnp.where(j == SEG_KR, DK_RET ** -0.5, 1.0).astype(F32)
        cos = cos_ref[...] * scale
        sin = sin_ref[...] * scale
        half = DK_RET // 2
        for hh in range(H_RET):
            a = hh * DK_RET
            x1 = acc[:, a:a + half]
            x2 = acc[:, a + half:a + DK_RET]
            h_ref[:, a:a + half] = (x1 * cos - x2 * sin).astype(h_ref.dtype)
            h_ref[:, a + half:a + DK_RET] = (x2 * cos + x1 * sin).astype(h_ref.dtype)
    else:
        scale = jnp.where(j == SEG_QA, q_scale, 1.0).astype(F32)
        h_ref[...] = (acc * scale).astype(h_ref.dtype)


def _in_proj_kernel(xb_ref, xs_ref, w_ref, cos_ref, sin_ref, cos_s_ref, sin_s_ref, *rest,
                    q_scale, q_scale_s, side_steps):
    n_side = len(side_steps)
    side_in = rest[:n_side]
    h_ref, hs_ref = rest[n_side:n_side + 2]
    side_out = rest[n_side + 2:2 * n_side + 2]
    wb_ref, = rest[2 * n_side + 2:]
    j = pl.program_id(0)
    i = pl.program_id(1)
    _side_cast_step((j - (SEG_KR + 1)) * pl.num_programs(1) + i, side_in, side_out, side_steps)

    @pl.when(i == 0)
    def _cast_weights():
        wb_ref[...] = w_ref[...].astype(BF16)

    def project(rope):
        acc = jnp.dot(xb_ref[...], wb_ref[...], preferred_element_type=F32)
        _store_segment(acc, j, cos_ref, sin_ref, h_ref, q_scale, rope)

        @pl.when(i == 0)
        def _sample_rows():
            acc_s = jnp.dot(xs_ref[...].astype(BF16), wb_ref[...], preferred_element_type=F32)
            _store_segment(acc_s, j, cos_s_ref, sin_s_ref, hs_ref, q_scale_s, rope)

    pl.when(j <= SEG_KR)(functools.partial(project, True))
    pl.when(j > SEG_KR)(functools.partial(project, False))


def _in_proj(xb2d, xs2d, w, cos, sin, cos_s, sin_s, side_weights, rows, q_scale, q_scale_s):
    m, d = xb2d.shape
    n_s = xs2d.shape[0]
    assert w.shape[1] == (N_SEG + 2) * SEG and m % rows == 0 and cos.shape[0] % rows == 0
    n_i = m // rows
    pos_blocks = cos.shape[0] // rows
    half = DK_RET // 2
    const2 = lambda j, i: (0, 0)
    w_seg = lambda j, i: (0, jnp.where(j < W_SEG_KA, j, j + 2))
    chunk, side_steps = _side_cast_plan(side_weights, (N_SEG - (SEG_KR + 1)) * n_i)

    def side_spec(t):
        first, count = side_steps[t]
        return pl.BlockSpec((chunk, side_weights[t].shape[1]),
                            lambda j, i: (jnp.clip((j - (SEG_KR + 1)) * n_i + i - first, 0, count - 1), 0))

    side_specs = [side_spec(t) for t in range(len(side_weights))]
    return pl.pallas_call(
        functools.partial(_in_proj_kernel, q_scale=q_scale, q_scale_s=q_scale_s, side_steps=side_steps),
        grid=(N_SEG, n_i),
        in_specs=[
            pl.BlockSpec((rows, d), lambda j, i: (i, 0)),
            pl.BlockSpec((n_s, d), const2),
            pl.BlockSpec((d, SEG), w_seg),
            pl.BlockSpec((rows, half), lambda j, i: (i % pos_blocks, 0)),
            pl.BlockSpec((rows, half), lambda j, i: (i % pos_blocks, 0)),
            pl.BlockSpec((n_s, half), const2),
            pl.BlockSpec((n_s, half), const2),
        ] + side_specs,
        out_specs=[
            pl.BlockSpec((rows, SEG), lambda j, i: (i, j)),
            pl.BlockSpec((n_s, SEG), lambda j, i: (0, j)),
        ] + side_specs,
        out_shape=[
            jax.ShapeDtypeStruct((m, N_SEG * SEG), BF16),
            jax.ShapeDtypeStruct((n_s, N_SEG * SEG), F32),
        ] + [jax.ShapeDtypeStruct(sw.shape, BF16) for sw in side_weights],
        scratch_shapes=[pltpu.VMEM((d, SEG), BF16)],
        compiler_params=_params("arbitrary", "arbitrary"),
        name="in_proj",
    )(xb2d, xs2d, w, cos, sin, cos_s, sin_s, *side_weights)


def _retention_prompt_kernel(lg_ref, q_ref, k_ref, v_ref, z_ref, y_ref, s_out_ref,
                             s_ref, decay_ref, cross_ref, kdec_ref):
    c = pl.program_id(1)
    chunk = q_ref.shape[0]

    @pl.when(c == 0)
    def _init():
        s_ref[...] = jnp.zeros_like(s_ref)
        row = lax.broadcasted_iota(jnp.int32, (chunk, chunk), 0)
        col = lax.broadcasted_iota(jnp.int32, (chunk, chunk), 1)
        diff = (row - col).astype(F32)
        pos = lax.broadcasted_iota(jnp.int32, (chunk, DV_RET), 0).astype(F32)
        for hh in range(H_RET):
            lg = lg_ref[hh]
            decay_ref[hh] = jnp.where(diff >= 0, jnp.exp(lg * jnp.maximum(diff, 0.0)), 0.0)
            cross_ref[hh] = jnp.exp(lg * (pos + 1.0))
            kdec_ref[hh] = jnp.exp(lg * (chunk - 1.0 - pos))

    for hh in range(H_RET):
        sl = slice(hh * DK_RET, (hh + 1) * DK_RET)
        q = q_ref[:, sl]
        k = k_ref[:, sl]
        v = v_ref[:, sl]
        s_old = s_ref[hh]
        intra = (_nt_dot(q, k) * decay_ref[hh]).astype(BF16)
        o = jnp.dot(intra, v, preferred_element_type=F32)
        o = o + jnp.dot(q, s_old.astype(BF16), preferred_element_type=F32) * cross_ref[hh]
        k_dec = (k.astype(F32) * kdec_ref[hh]).astype(BF16)
        state_decay = jnp.exp(jnp.full((1, DV_RET), lg_ref[hh] * chunk, F32))
        s_new = s_old * state_decay + lax.dot_general(
            k_dec, v, (((0,), (0,)), ((), ())), preferred_element_type=F32)
        s_ref[hh] = s_new
        mu = jnp.mean(o, axis=-1, keepdims=True)
        cen = o - mu
        var = jnp.mean(cen * cen, axis=-1, keepdims=True)
        gate = _silu(z_ref[:, sl].astype(F32))
        y_ref[:, sl] = (cen * lax.rsqrt(var + GN_EPS) * gate).astype(y_ref.dtype)

    @pl.when(c == pl.num_programs(1) - 1)
    def _emit_state():
        s_out_ref[...] = s_ref[...]


def _retention_prompt(h, log_gamma, batch, seq):
    chunk = RET_CHUNK
    assert seq % chunk == 0
    nc = seq // chunk
    blk = lambda seg: pl.BlockSpec((chunk, SEG), lambda b, c, seg=seg: (b * nc + c, seg))
    return pl.pallas_call(
        _retention_prompt_kernel,
        grid=(batch, nc),
        in_specs=[pl.BlockSpec(memory_space=pltpu.SMEM),
                  blk(SEG_QR), blk(SEG_KR), blk(SEG_VR), blk(SEG_ZR)],
        out_specs=[
            pl.BlockSpec((chunk, W_RET), lambda b, c: (b * nc + c, 0)),
            pl.BlockSpec((None, H_RET, DK_RET, DV_RET), lambda b, c: (b, 0, 0, 0)),
        ],
        out_shape=[
            jax.ShapeDtypeStruct((batch * seq, W_RET), BF16),
            jax.ShapeDtypeStruct((batch, H_RET, DK_RET, DV_RET), F32),
        ],
        scratch_shapes=[
            pltpu.VMEM((H_RET, DK_RET, DV_RET), F32),
            pltpu.VMEM((H_RET, chunk, chunk), F32),
            pltpu.VMEM((H_RET, chunk, DV_RET), F32),
            pltpu.VMEM((H_RET, chunk, DK_RET), F32),
        ],
        compiler_params=_params("arbitrary", "arbitrary"),
        name="retention_prompt",
    )(log_gamma, h, h, h, h)


def _diff_lambda_in_kernel(lq1_ref, lk1_ref, lq2_ref, lk2_ref, lam_init):
    a = jnp.sum(lq1_ref[...] * lk1_ref[...], axis=-1, keepdims=True)
    b = jnp.sum(lq2_ref[...] * lk2_ref[...], axis=-1, keepdims=True)
    return jnp.exp(a) - jnp.exp(b) + lam_init


def _diff_prompt_kernel(lq1_ref, lk1_ref, lq2_ref, lk2_ref, gain_ref, q_ref, k_ref, v_ref, z_ref, y_ref,
                        sa_ref, sb_ref, ma_ref, mb_ref, m_ref, l_ref, acc_ref, *, lam_init):
    i = pl.program_id(2)
    blk = q_ref.shape[0]
    lanes = m_ref.shape[-1]
    qs = (q_ref[:, :DK_DIFF], q_ref[:, DK_DIFF:])

    m_ref[...] = jnp.full_like(m_ref, NEG_INF)
    l_ref[...] = jnp.zeros_like(l_ref)
    acc_ref[...] = jnp.zeros_like(acc_ref)

    def scores_into(bufs, j):
        buf_ref, bmax_ref = bufs
        start = pl.multiple_of(j * blk, blk)
        kb = k_ref[pl.ds(start, blk), :]
        for c in range(2):
            s = _nt_dot(qs[c], kb[:, c * DK_DIFF:(c + 1) * DK_DIFF])
            buf_ref[c] = s
            bmax_ref[c] = jnp.broadcast_to(jnp.max(s, axis=-1, keepdims=True), (blk, lanes))

    def softmax_pv(bufs, j, masked):
        buf_ref, bmax_ref = bufs
        start = pl.multiple_of(j * blk, blk)
        vb = v_ref[pl.ds(start, blk), :]
        if masked:
            row = lax.broadcasted_iota(jnp.int32, (blk, blk), 0)
            col = lax.broadcasted_iota(jnp.int32, (blk, blk), 1)
            keep = col <= row
        for c in range(2):
            s = buf_ref[c]
            m_old = m_ref[c]
            if masked:
                s = jnp.where(keep, s, NEG_INF)
                m_new = jnp.maximum(m_old, jnp.max(s, axis=-1, keepdims=True))
            else:
                m_new = jnp.maximum(m_old, bmax_ref[c])
            alpha = jnp.exp2(m_old - m_new)
            p = jnp.exp2(s - jnp.concatenate([m_new] * (blk // lanes), axis=1))
            part = p[:, 0:lanes]
            for t in range(1, blk // lanes):
                part = part + p[:, t * lanes:(t + 1) * lanes]
            l_ref[c] = alpha * l_ref[c] + part
            acc_ref[c] = (jnp.concatenate([alpha] * (DV_DIFF // lanes), axis=1) * acc_ref[c]
                          + jnp.dot(p.astype(BF16), vb, preferred_element_type=F32))
            m_ref[c] = m_new

    buf_a = (sa_ref, ma_ref)
    buf_b = (sb_ref, mb_ref)
    scores_into(buf_a, 0)

    def pair(t, carry):
        j = 2 * t
        scores_into(buf_b, j + 1)
        softmax_pv(buf_a, j, False)
        scores_into(buf_a, j + 2)
        softmax_pv(buf_b, j + 1, False)
        return carry

    lax.fori_loop(0, i // 2, pair, 0)

    @pl.when(i % 2 == 1)
    def _odd_tail():
        scores_into(buf_b, i)
        softmax_pv(buf_a, i - 1, False)
        softmax_pv(buf_b, i, True)

    @pl.when(i % 2 == 0)
    def _even_tail():
        softmax_pv(buf_a, i, True)

    lam = _diff_lambda_in_kernel(lq1_ref, lk1_ref, lq2_ref, lk2_ref, lam_init)
    l0 = jnp.sum(l_ref[0], axis=-1, keepdims=True)
    l1 = jnp.sum(l_ref[1], axis=-1, keepdims=True)
    o = acc_ref[0] / l0 - lam * (acc_ref[1] / l1)
    o = o * lax.rsqrt(jnp.mean(o * o, axis=-1, keepdims=True) + GN_EPS)
    o = o * gain_ref[...] * (1.0 - lam_init)
    y_ref[...] = (o * _silu(z_ref[...].astype(F32))).astype(y_ref.dtype)


def _diff_prompt(h, hkv, lam_params, gain, lam_init, batch, seq):
    blk = Q_BLOCK
    assert seq % blk == 0
    nq = seq // blk
    heads_per_seg = SEG // DV_DIFF
    small = pl.BlockSpec((1, DK_DIFF), lambda b, hh, i: (0, 0))
    return pl.pallas_call(
        functools.partial(_diff_prompt_kernel, lam_init=lam_init),
        grid=(batch, H_DIFF, nq),
        in_specs=[
            small, small, small, small,
            pl.BlockSpec((1, DV_DIFF), lambda b, hh, i: (0, 0)),
            pl.BlockSpec((blk, 2 * DK_DIFF), lambda b, hh, i: (b * nq + i, SEG_QA * heads_per_seg + hh)),
            pl.BlockSpec((seq, 2 * DK_DIFF), lambda b, hh, i: (b, hh)),
            pl.BlockSpec((seq, DV_DIFF), lambda b, hh, i: (b, heads_per_seg + hh)),
            pl.BlockSpec((blk, DV_DIFF), lambda b, hh, i: (b * nq + i, SEG_ZA * heads_per_seg + hh)),
        ],
        out_specs=pl.BlockSpec((blk, DV_DIFF), lambda b, hh, i: (b * nq + i, hh)),
        out_shape=jax.ShapeDtypeStruct((batch * seq, W_DIFF), BF16),
        scratch_shapes=[
            pltpu.VMEM((2, blk, blk), F32),
            pltpu.VMEM((2, blk, blk), F32),
            pltpu.VMEM((2, blk, LANES), F32),
            pltpu.VMEM((2, blk, LANES), F32),
            pltpu.VMEM((2, blk, LANES), F32),
            pltpu.VMEM((2, blk, LANES), F32),
            pltpu.VMEM((2, blk, DV_DIFF), F32),
        ],
        compiler_params=_params("arbitrary", "arbitrary", "arbitrary"),
        name="diff_prompt",
    )(*lam_params, gain, h, hkv, hkv, h)


def _retention_sample_kernel(lg_ref, q_ref, k_ref, kt_ref, v_ref, vall_ref, z_ref, s_in_ref,
                             y_ref, s_out_ref):
    g = pl.program_id(0)
    hh = pl.program_id(1)
    group = q_ref.shape[0]
    n_seq = kt_ref.shape[1]
    gamma = jnp.exp(jnp.full((1, DV_RET), lg_ref[hh], F32))
    q_bf = q_ref[...].astype(BF16)
    k_bf = k_ref[...].astype(BF16)
    v_bf = v_ref[...].astype(BF16)
    qk = jnp.sum(q_bf.astype(F32) * k_bf.astype(F32), axis=-1, keepdims=True)
    kt_bf = kt_ref[...].astype(BF16)
    vall_bf = vall_ref[...].astype(BF16)
    seq_lane = lax.broadcasted_iota(jnp.int32, (DK_RET, n_seq), 1)
    row = lax.broadcasted_iota(jnp.int32, (group, DV_RET), 0)
    cross = jnp.zeros((group, DV_RET), F32)
    for s in range(group):
        s_old = s_in_ref[s]
        qs = jnp.dot(q_bf, s_old.astype(BF16), preferred_element_type=F32)
        cross = jnp.where(row == s, qs, cross)
        k_only = jnp.where(seq_lane == g * group + s, kt_bf, jnp.zeros_like(kt_bf))
        outer = jnp.dot(k_only, vall_bf, preferred_element_type=F32)
        s_out_ref[s] = s_old * gamma + outer
    o = qk * v_bf.astype(F32) + cross * gamma
    mu = jnp.mean(o, axis=-1, keepdims=True)
    cen = o - mu
    var = jnp.mean(cen * cen, axis=-1, keepdims=True)
    y_ref[...] = (cen * lax.rsqrt(var + GN_EPS) * _silu(z_ref[...].astype(F32))).astype(y_ref.dtype)


def _retention_sample(hs, kt, state, log_gamma):
    n_seq = hs.shape[0]
    group = SAMPLE_GROUP
    assert n_seq % group == 0
    per_seg = SEG // DK_RET
    col = lambda seg: pl.BlockSpec((group, DK_RET), lambda g, hh, seg=seg: (g, seg * per_seg + hh))
    st = pl.BlockSpec((group, None, DK_RET, DV_RET), lambda g, hh: (g, hh, 0, 0))
    return pl.pallas_call(
        _retention_sample_kernel,
        grid=(n_seq // group, H_RET),
        in_specs=[
            pl.BlockSpec(memory_space=pltpu.SMEM),
            col(SEG_QR), col(SEG_KR),
            pl.BlockSpec((DK_RET, n_seq), lambda g, hh: (hh, 0)),
            col(SEG_VR),
            pl.BlockSpec((n_seq, DV_RET), lambda g, hh: (0, SEG_VR * per_seg + hh)),
            col(SEG_ZR),
            st,
        ],
        out_specs=[pl.BlockSpec((group, DV_RET), lambda g, hh: (g, hh)), st],
        out_shape=[
            jax.ShapeDtypeStruct((n_seq, W_RET), F32),
            jax.ShapeDtypeStruct(state.shape, F32),
        ],
        compiler_params=_params("arbitrary", "arbitrary"),
        name="retention_sample",
    )(log_gamma, hs, hs, kt, hs, hs, hs, state)


def _diff_sample_kernel(pt_ref, lq1_ref, lk1_ref, lq2_ref, lk2_ref, gain_ref, q_ref, kn_ref, vn_ref, z_ref,
                        ck_hbm, cv_hbm, y_ref, kbuf_ref, vbuf_ref, s_ref, bias_ref, ksem_ref, vsem_ref,
                        *, n_seq, layer, lam_init):
    b = pl.program_id(0)
    n_slots, n_pages, page = kbuf_ref.shape[0], kbuf_ref.shape[1], kbuf_ref.shape[2]
    rows = page * H_DIFF
    lanes = bias_ref.shape[-1]

    def page_copies(seq, slot):
        out = []
        for p in range(n_pages):
            src = pt_ref[seq, p]
            out.append(pltpu.make_async_copy(ck_hbm.at[layer, src], kbuf_ref.at[slot, p], ksem_ref.at[slot]))
            out.append(pltpu.make_async_copy(cv_hbm.at[layer, src], vbuf_ref.at[slot, p], vsem_ref.at[slot]))
        return out

    def request(seq):
        for cp in page_copies(seq, seq % n_slots):
            cp.start()

    @pl.when(b == 0)
    def _first_requests():
        for seq in range(min(n_slots - 1, n_seq)):
            request(seq)

    @pl.when(b + (n_slots - 1) < n_seq)
    def _request_ahead():
        request(b + (n_slots - 1))

    slot = b % n_slots
    for cp in page_copies(b, slot):
        cp.wait()
    k_pages = [kbuf_ref.at[slot, p] for p in range(n_pages)]
    v_pages = [vbuf_ref.at[slot, p] for p in range(n_pages)]

    @pl.when(b == 0)
    def _init_bias():
        r = lax.broadcasted_iota(jnp.int32, (rows, lanes), 0)
        j = lax.broadcasted_iota(jnp.int32, (rows, lanes), 1)
        bias_ref[...] = jnp.where((j < 2 * H_DIFF) & (r % H_DIFF == j % H_DIFF), 0.0, NEG_INF)

    q4 = q_ref[...]
    in_map0 = lax.broadcasted_iota(jnp.int32, q4.shape, 1) < DK_DIFF
    q_cols = jnp.concatenate(
        [jnp.where(in_map0, q4, 0.0), jnp.where(in_map0, 0.0, q4),
         jnp.zeros((lanes - 2 * H_DIFF, 2 * DK_DIFF), F32)], axis=0).astype(BF16)

    kn2 = jnp.concatenate([kn_ref[...], kn_ref[...]], axis=0)
    new_rows = kn2.shape[0]
    new_bias = jnp.where(lax.broadcasted_iota(jnp.int32, (new_rows, lanes), 0) < H_DIFF,
                         bias_ref[0:new_rows, :], NEG_INF)
    s_new = _nt_dot(kn2.astype(BF16), q_cols) + new_bias
    m = jnp.max(s_new, axis=0, keepdims=True)
    for p in range(n_pages):
        k_mat = k_pages[p][...].reshape(rows, 2 * DK_DIFF).astype(BF16)
        s = _nt_dot(k_mat, q_cols) + bias_ref[...]
        s_ref[p] = s
        m = jnp.maximum(m, jnp.max(s, axis=0, keepdims=True))
    e_new = jnp.exp(s_new - m)
    l = jnp.sum(e_new, axis=0, keepdims=True)
    for p in range(n_pages):
        e = jnp.exp(s_ref[p] - m)
        s_ref[p] = e
        l = l + jnp.sum(e, axis=0, keepdims=True)
    lam = _diff_lambda_in_kernel(lq1_ref, lk1_ref, lq2_ref, lk2_ref, lam_init)
    lane = lax.broadcasted_iota(jnp.int32, (1, lanes), 1)
    coef = jnp.where(lane < H_DIFF, 1.0, -lam) / l
    coef = jnp.where(lane < 2 * H_DIFF, coef, 0.0)
    spread = jnp.ones((lanes, DV_DIFF), BF16)
    a_new = jnp.dot((e_new * coef).astype(BF16), spread, preferred_element_type=F32)
    acc = a_new * jnp.concatenate([vn_ref[...], vn_ref[...]], axis=0)
    for p in range(n_pages):
        a = jnp.dot((s_ref[p] * coef).astype(BF16), spread, preferred_element_type=F32)
        av = a * v_pages[p][...].reshape(rows, DV_DIFF)
        acc = acc + jnp.sum(av.reshape(rows // new_rows, new_rows, DV_DIFF), axis=0)
    o = acc[0:H_DIFF, :] + acc[H_DIFF:new_rows, :]
    o = o * lax.rsqrt(jnp.mean(o * o, axis=-1, keepdims=True) + GN_EPS)
    o = o * gain_ref[...] * (1.0 - lam_init)
    y_ref[...] = (o * _silu(z_ref[...])).astype(y_ref.dtype)


def _diff_sample(q4, kn4, vn4, z4, cache_k, cache_v, layer, page_table, lam_params, gain, lam_init):
    n_seq = q4.shape[0]
    n_pages = page_table.shape[1]
    page = cache_k.shape[2]
    lanes = 128
    small = pl.BlockSpec((1, DK_DIFF), lambda b, pt: (0, 0))
    tok = pl.BlockSpec((None, H_DIFF, DV_DIFF), lambda b, pt: (b, 0, 0))
    in_hbm = pl.BlockSpec(memory_space=pl.ANY)
    page_slots = pltpu.VMEM((PAGE_SLOTS, n_pages, page, H_DIFF, DV_DIFF), F32)
    grid_spec = pltpu.PrefetchScalarGridSpec(
        num_scalar_prefetch=1,
        grid=(n_seq,),
        in_specs=[small, small, small, small,
                  pl.BlockSpec((1, DV_DIFF), lambda b, pt: (0, 0)),
                  tok, tok, tok, tok, in_hbm, in_hbm],
        out_specs=tok,
        scratch_shapes=[page_slots, page_slots,
                        pltpu.VMEM((n_pages, page * H_DIFF, lanes), F32),
                        pltpu.VMEM((page * H_DIFF, lanes), F32),
                        pltpu.SemaphoreType.DMA((PAGE_SLOTS,)),
                        pltpu.SemaphoreType.DMA((PAGE_SLOTS,))],
    )
    return pl.pallas_call(
        functools.partial(_diff_sample_kernel, n_seq=n_seq, layer=layer, lam_init=lam_init),
        grid_spec=grid_spec,
        out_shape=jax.ShapeDtypeStruct((n_seq, H_DIFF, DV_DIFF), F32),
        compiler_params=pltpu.CompilerParams(dimension_semantics=("arbitrary",),
                                             vmem_limit_bytes=DECODE_VMEM_LIMIT_BYTES),
        name="diff_sample",
    )(page_table, *lam_params, gain, q4, kn4, vn4, z4, cache_k, cache_v)


def _merge_rows(x_ref, yr_ref, ya_ref, gr_ref, ga_ref, wr_ref, wa_ref, wo_ref, lng_ref, lnb_ref, y_ref, alpha):
    rows = x_ref.shape[0]
    part = max(rows // MERGE_PARTS, min(rows, MERGE_MIN_PART))
    for t in range(rows // part):
        rs = slice(t * part, (t + 1) * part)
        br = jnp.dot(yr_ref[rs, :].astype(BF16), wr_ref[...], preferred_element_type=F32)
        ba = jnp.dot(ya_ref[rs, :].astype(BF16), wa_ref[...], preferred_element_type=F32)
        m = _sigmoid(gr_ref[rs, :].astype(F32)) * br + _sigmoid(ga_ref[rs, :].astype(F32)) * ba
        h = alpha * x_ref[rs, :] + jnp.dot(m.astype(BF16), wo_ref[...], preferred_element_type=F32)
        mu = jnp.mean(h, axis=-1, keepdims=True)
        cen = h - mu
        var = jnp.mean(cen * cen, axis=-1, keepdims=True)
        y_ref[rs, :] = cen * lax.rsqrt(var + LN_EPS) * lng_ref[...] + lnb_ref[...]


def _merge_kernel(x_ref, yr_ref, ya_ref, gr_ref, ga_ref, xs_ref, yrs_ref, yas_ref, grs_ref, gas_ref,
                  wr_ref, wa_ref, wo_ref, lng_ref, lnb_ref, y_ref, ys_ref, *, alpha):
    weights = (wr_ref, wa_ref, wo_ref, lng_ref, lnb_ref)
    _merge_rows(x_ref, yr_ref, ya_ref, gr_ref, ga_ref, *weights, y_ref, alpha)

    @pl.when(pl.program_id(0) == 0)
    def _sample_rows():
        _merge_rows(xs_ref, yrs_ref, yas_ref, grs_ref, gas_ref, *weights, ys_ref, alpha)


def _merge(x2d, y_r, y_a, h, xs2d, y_rs, y_as, hs, w_r, w_a, w_o, ln_gain, ln_bias, rows, alpha):
    m, d = x2d.shape
    n_s = xs2d.shape[0]
    assert m % rows == 0
    gate_blocks = (2 * SEG) // d if d <= 2 * SEG else None
    assert gate_blocks == 1, "gates are two segments wide, same as the model width"
    const = lambda shape, col=0: pl.BlockSpec(shape, lambda i: (0, col), pipeline_mode=pl.Buffered(1))
    return pl.pallas_call(
        functools.partial(_merge_kernel, alpha=alpha),
        grid=(m // rows,),
        in_specs=[
            pl.BlockSpec((rows, d), lambda i: (i, 0)),
            pl.BlockSpec((rows, W_RET), lambda i: (i, 0)),
            pl.BlockSpec((rows, W_DIFF), lambda i: (i, 0)),
            pl.BlockSpec((rows, d), lambda i: (i, SEG_GR * SEG // d)),
            pl.BlockSpec((rows, d), lambda i: (i, SEG_GA * SEG // d)),
            const((n_s, d)), const((n_s, W_RET)), const((n_s, W_DIFF)),
            const((n_s, d), SEG_GR * SEG // d), const((n_s, d), SEG_GA * SEG // d),
            const((W_RET, d)), const((W_DIFF, d)), const((d, d)),
            const((1, d)), const((1, d)),
        ],
        out_specs=[pl.BlockSpec((rows, d), lambda i: (i, 0)),
                   pl.BlockSpec((n_s, d), lambda i: (0, 0))],
        out_shape=[jax.ShapeDtypeStruct((m, d), F32), jax.ShapeDtypeStruct((n_s, d), F32)],
        compiler_params=_params("arbitrary"),
        name="merge",
    )(x2d, y_r, y_a, h, h, xs2d, y_rs, y_as, hs, hs, w_r, w_a, w_o, ln_gain, ln_bias)


def _rope_tables(pos):
    half = DK_RET // 2
    inv_freq = 1.0 / (ROPE_BASE ** (np.arange(half, dtype=np.float64) / half))
    ang = np.asarray(pos, np.float64)[:, None] * inv_freq[None, :]
    return jnp.asarray(np.cos(ang), F32), jnp.asarray(np.sin(ang), F32)


def kernel(x_prompt, x_sample, state_ret, cache_k, cache_v, page_table, w_in, w_branch_ret, w_branch_diff, w_out, lambda_q1, lambda_k1, lambda_q2, lambda_k2, subln_gain, ln_gain, ln_bias):
    batch, seq, d_model = x_prompt.shape
    n_seq, t_s, _ = x_sample.shape
    depth = w_in.shape[0]
    assert t_s == 1, "the sample kernels implement the one-new-token step"
    n_pool, page = cache_k.shape[1], cache_k.shape[2]
    past_len = page_table.shape[1] * page
    alpha = (2.0 * depth) ** 0.25

    log_gamma = jnp.asarray(np.log(1.0 - np.exp2(-5.0 - np.arange(H_RET, dtype=np.float64))), F32)
    cos_p, sin_p = _rope_tables(np.arange(seq))
    cos_s, sin_s = _rope_tables(np.full((n_seq,), past_len))

    hp = x_prompt.reshape(batch * seq, d_model)
    hs = x_sample.reshape(n_seq, d_model)
    outs = [[] for _ in range(6)]
    for l in range(depth):
        lam_init = 0.8 - 0.6 * math.exp(-0.3 * l)
        lam_params = [p[l].reshape(1, DK_DIFF) for p in (lambda_q1, lambda_k1, lambda_q2, lambda_k2)]
        gain = subln_gain[l].reshape(1, DV_DIFF)
        lng = ln_gain[l].reshape(1, d_model)
        lnb = ln_bias[l].reshape(1, d_model)

        hkv_p, k_p, v_p, hp_bf, k_s, v_s = _kv_proj(hp, hs, w_in[l], KV_PROJ_ROWS)
        h_p, h_s, w_r_bf, w_a_bf, w_o_bf = _in_proj(
            hp_bf, hs, w_in[l], cos_p, sin_p, cos_s, sin_s, [w_branch_ret[l], w_branch_diff[l], w_out[l]],
            PROJ_ROWS, DK_DIFF ** -0.5 * LOG2E, DK_DIFF ** -0.5)

        y_r, s_p = _retention_prompt(h_p, log_gamma, batch, seq)
        y_a = _diff_prompt(h_p, hkv_p, lam_params, gain, lam_init, batch, seq)

        kt = h_s[:, SEG_KR * SEG:(SEG_KR + 1) * SEG].T
        y_rs, s_s = _retention_sample(h_s, kt, state_ret[l], log_gamma)
        tok = lambda seg: h_s[:, seg * SEG:(seg + 1) * SEG].reshape(n_seq, H_DIFF, DV_DIFF)
        y_as = _diff_sample(tok(SEG_QA), k_s, v_s, tok(SEG_ZA), cache_k, cache_v, l,
                            page_table, lam_params, gain, lam_init)

        hp, hs = _merge(hp, y_r, y_a, h_p, hs, y_rs, y_as.reshape(n_seq, W_DIFF), h_s,
                        w_r_bf, w_a_bf, w_o_bf, lng, lnb, MERGE_ROWS, alpha)

        outs[0].append(s_p)
        outs[1].append(k_p.reshape(batch, seq, H_DIFF, 2 * DK_DIFF))
        outs[2].append(v_p.reshape(batch, seq, H_DIFF, DV_DIFF))
        outs[3].append(s_s)
        outs[4].append(k_s.reshape(n_seq, t_s, H_DIFF, 2 * DK_DIFF))
        outs[5].append(v_s.reshape(n_seq, t_s, H_DIFF, DV_DIFF))

    return (hp.reshape(batch, seq, d_model), hs.reshape(n_seq, t_s, d_model),
            *[jnp.stack(o) for o in outs])
```
